```python
import jax, jax.numpy as jnp
from jax import lax
import numpy as np

D_MODEL = 2048
BATCH = 1
SEQ = 8192
DEPTH = 1

PLE_DIM = 256
RMS_EPS = 1e-6
N_HEADS = 16
QK_NOPE_DIM = 128
QK_ROPE_DIM = 64
V_HEAD_DIM = 128
QK_HEAD_DIM = QK_NOPE_DIM + QK_ROPE_DIM
Q_LORA_RANK = 768
KV_LORA_RANK = 512
ROPE_THETA = 10000.0
Q_BLOCK = 128
D_RNN = D_MODEL
RNN_BLOCKS = 16
RNN_BLOCK_DIM = D_RNN // RNN_BLOCKS
CONV_WIDTH = 4
LRU_C = 8.0
PEER_HEADS = 8
N_KEYS = 128
N_EXPERTS = N_KEYS * N_KEYS
PEER_QUERY_DIM = 256
PEER_HALF = PEER_QUERY_DIM // 2
PEER_TOPK = 16
PEER_CHUNK = 128
IN_SPLITS = (Q_LORA_RANK, KV_LORA_RANK, QK_ROPE_DIM, D_RNN, D_RNN, D_MODEL, D_MODEL)
IN_WIDTH = Q_LORA_RANK + KV_LORA_RANK + QK_ROPE_DIM + 2 * D_RNN + 2 * D_MODEL

kernel_name = 'hybrid_mla_rglru_peer_block'


def rms_norm(x, g):
    xf = x.astype(jnp.float32)
    y = xf * lax.rsqrt(jnp.mean(xf * xf, axis=-1, keepdims=True) + RMS_EPS)
    return (y * g.astype(jnp.float32)).astype(x.dtype)


def rope(x, pos):
    half = QK_ROPE_DIM // 2
    inv_freq = 1.0 / (ROPE_THETA ** (jnp.arange(half, dtype=jnp.float32) / half))
    ang = pos.astype(jnp.float32)[:, None] * inv_freq[None, :]
    cos = jnp.cos(ang)[:, None, :]
    sin = jnp.sin(ang)[:, None, :]
    xf = x.astype(jnp.float32)
    x1, x2 = xf[..., :half], xf[..., half:]
    return jnp.concatenate([x1 * cos - x2 * sin, x2 * cos + x1 * sin], axis=-1).astype(x.dtype)


def split_columns(z):
    offs, acc = [], 0
    for w in IN_SPLITS[:-1]:
        acc += w
        offs.append(acc)
    return jnp.split(z, offs, axis=-1)


def mla_branch(c_q, c_kv, k_r, q_norm, w_uq, kv_norm, w_ukv, w_attn_o):
    B, S, _ = c_q.shape
    pos = jnp.arange(S)
    q = (rms_norm(c_q, q_norm) @ w_uq).reshape(B, S, N_HEADS, QK_HEAD_DIM)
    q_nope = q[..., :QK_NOPE_DIM]
    q_pe = rope(q[..., QK_NOPE_DIM:], pos)
    kv = (rms_norm(c_kv, kv_norm) @ w_ukv).reshape(B, S, N_HEADS, QK_NOPE_DIM + V_HEAD_DIM)
    k_nope, v = kv[..., :QK_NOPE_DIM], kv[..., QK_NOPE_DIM:]
    k_pe = rope(k_r[:, :, None, :], pos)[:, :, 0, :]
    scale = QK_HEAD_DIM ** -0.5
    n_blk = S // Q_BLOCK

    def to_blocks(t):
        return t.reshape(B, n_blk, Q_BLOCK, *t.shape[2:]).swapaxes(0, 1)

    def attend(args):
        qn, qp, blk = args
        s = (jnp.einsum('bqhd,bkhd->bhqk', qn, k_nope, preferred_element_type=jnp.float32)
             + jnp.einsum('bqhd,bkd->bhqk', qp, k_pe, preferred_element_type=jnp.float32))
        q_pos = blk * Q_BLOCK + jnp.arange(Q_BLOCK)
        s = jnp.where(pos[None, :] <= q_pos[:, None], s * scale, -jnp.inf)
        prob = jax.nn.softmax(s, axis=-1)
        return jnp.einsum('bhqk,bkhd->bqhd', prob.astype(v.dtype), v)

    o = lax.map(attend, (to_blocks(q_nope), to_blocks(q_pe), jnp.arange(n_blk)))
    o = o.swapaxes(0, 1).reshape(B, S, N_HEADS * V_HEAD_DIM)
    return o @ w_attn_o


def rglru_branch(x_r, y_r, conv_w, conv_b, w_rg_a, b_rg_a, w_rg_x, b_rg_x, lru_lambda, w_rnn_o):
    B, S, C = x_r.shape
    xc = lax.conv_general_dilated(x_r, conv_w[:, None, :], window_strides=(1,),
                                  padding=[(CONV_WIDTH - 1, 0)],
                                  dimension_numbers=('NWC', 'WIO', 'NWC'),
                                  feature_group_count=C) + conv_b
    xb = xc.reshape(B, S, RNN_BLOCKS, RNN_BLOCK_DIM)
    r = jax.nn.sigmoid(jnp.einsum('bshi,hij->bshj', xb, w_rg_a) + b_rg_a).reshape(B, S, C)
    i = jax.nn.sigmoid(jnp.einsum('bshi,hij->bshj', xb, w_rg_x) + b_rg_x).reshape(B, S, C)
    log_a = -LRU_C * r.astype(jnp.float32) * jax.nn.softplus(-lru_lambda.astype(jnp.float32))
    a = jnp.exp(log_a)
    mult = jnp.sqrt(-jnp.expm1(2.0 * log_a))
    mult = jnp.where((jnp.arange(S) == 0)[None, :, None], 1.0, mult)
    b = mult * (i * xc).astype(jnp.float32)

    def combine(left, right):
        a1, b1 = left
        a2, b2 = right
        return a1 * a2, a2 * b1 + b2

    _, h = lax.associative_scan(combine, (a, b), axis=1)
    y = h.astype(x_r.dtype) * jax.nn.gelu(y_r)
    return y @ w_rnn_o


def peer_ffn(x, w_peer_q, peer_subkeys, peer_u, peer_v):
    B, S, D = x.shape
    q = (x @ w_peer_q).reshape(B, S, PEER_HEADS, 2, PEER_HALF)
    s = jnp.einsum('bshcd,hcnd->bshcn', q, peer_subkeys, preferred_element_type=jnp.float32)
    top_s, top_i = lax.top_k(s, PEER_TOPK)
    cand = top_s[..., 0, :, None] + top_s[..., 1, None, :]
    cand = cand.reshape(B, S, PEER_HEADS, PEER_TOPK * PEER_TOPK)
    best_s, best_c = lax.top_k(cand, PEER_TOPK)
    i1 = jnp.take_along_axis(top_i[..., 0, :], best_c // PEER_TOPK, axis=-1)
    i2 = jnp.take_along_axis(top_i[..., 1, :], best_c % PEER_TOPK, axis=-1)
    experts = i1 * N_KEYS + i2
    gates = jax.nn.softmax(best_s, axis=-1).astype(x.dtype)
    T = B * S
    HK = PEER_HEADS * PEER_TOPK
    xs = x.reshape(T // PEER_CHUNK, PEER_CHUNK, D)
    es = experts.reshape(T // PEER_CHUNK, PEER_CHUNK, HK)
    gs = gates.reshape(T // PEER_CHUNK, PEER_CHUNK, HK)

    def chunk(args):
        xc, ec, gc = args
        u = peer_u[ec]
        act = jax.nn.gelu(jnp.einsum('tkd,td->tk', u, xc))
        return jnp.einsum('tk,tkd->td', gc * act, peer_v[ec])

    return lax.map(chunk, (xs, es, gs)).reshape(B, S, D)


def setup_inputs(seed: int = 0) -> dict:
    key = jax.random.key(seed)
    ks = iter(jax.random.split(key, 40))
    f32 = jnp.float32

    def nrm(shape, scale):
        return jax.random.normal(next(ks), shape, f32) * scale

    def gain(shape):
        return 1.0 + nrm(shape, 0.05)

    L = DEPTH
    u = jax.random.uniform(next(ks), (L, D_RNN), f32, 0.9, 0.999)
    s_root = u ** (1.0 / LRU_C)
    lru_lambda = jnp.log(s_root) - jnp.log1p(-s_root)
    return {
        'x': nrm((BATCH, SEQ, D_MODEL), 1.0),
        'p': nrm((DEPTH, BATCH, SEQ, PLE_DIM), 1.0),
        'attn_norm': gain((L, D_MODEL)),
        'w_in': nrm((L, D_MODEL, IN_WIDTH), D_MODEL ** -0.5),
        'b_gate': nrm((L, 2 * D_MODEL), 0.01),
        'q_norm': gain((L, Q_LORA_RANK)),
        'w_uq': nrm((L, Q_LORA_RANK, N_HEADS * QK_HEAD_DIM), Q_LORA_RANK ** -0.5),
        'kv_norm': gain((L, KV_LORA_RANK)),
        'w_ukv': nrm((L, KV_LORA_RANK, N_HEADS * (QK_NOPE_DIM + V_HEAD_DIM)), KV_LORA_RANK ** -0.5),
        'w_attn_o': nrm((L, N_HEADS * V_HEAD_DIM, D_MODEL), (N_HEADS * V_HEAD_DIM) ** -0.5),
        'conv_w': nrm((L, CONV_WIDTH, D_RNN), CONV_WIDTH ** -0.5),
        'conv_b': nrm((L, D_RNN), 0.01),
        'w_rg_a': nrm((L, RNN_BLOCKS, RNN_BLOCK_DIM, RNN_BLOCK_DIM), RNN_BLOCK_DIM ** -0.5),
        'b_rg_a': nrm((L, RNN_BLOCKS, RNN_BLOCK_DIM), 0.01),
        'w_rg_x': nrm((L, RNN_BLOCKS, RNN_BLOCK_DIM, RNN_BLOCK_DIM), RNN_BLOCK_DIM ** -0.5),
        'b_rg_x': nrm((L, RNN_BLOCKS, RNN_BLOCK_DIM), 0.01),
        'lru_lambda': lru_lambda,
        'w_rnn_o': nrm((L, D_RNN, D_MODEL), D_RNN ** -0.5),
        'w_out': nrm((L, D_MODEL, D_MODEL), D_MODEL ** -0.5),
        'ffn_norm': gain((L, D_MODEL)),
        'w_peer_q': nrm((L, D_MODEL, PEER_HEADS * PEER_QUERY_DIM), D_MODEL ** -0.5),
        'peer_subkeys': nrm((L, PEER_HEADS, 2, N_KEYS, PEER_HALF), PEER_HALF ** -0.5),
        'peer_u': nrm((L, N_EXPERTS, D_MODEL), D_MODEL ** -0.5),
        'peer_v': nrm((L, N_EXPERTS, D_MODEL), PEER_HEADS ** -0.5),
        'ple_norm': gain((L, D_MODEL)),
        'w_ple_gate': nrm((L, D_MODEL, D_MODEL), D_MODEL ** -0.5),
        'w_ple_proj': nrm((L, PLE_DIM, D_MODEL), PLE_DIM ** -0.5),
        'final_norm': gain((D_MODEL,)),
    }


def reference(x, p, attn_norm, w_in, b_gate, q_norm, w_uq, kv_norm, w_ukv, w_attn_o,
              conv_w, conv_b, w_rg_a, b_rg_a, w_rg_x, b_rg_x, lru_lambda, w_rnn_o, w_out,
              ffn_norm, w_peer_q, peer_subkeys, peer_u, peer_v, ple_norm, w_ple_gate,
              w_ple_proj, final_norm):
    h = x
    for l in range(DEPTH):
        n = rms_norm(h, attn_norm[l])
        c_q, c_kv, k_r, x_r, y_r, ga, gr = split_columns(n @ w_in[l])
        g_attn = jax.nn.sigmoid(ga + b_gate[l, :D_MODEL])
        g_rnn = jax.nn.sigmoid(gr + b_gate[l, D_MODEL:])
        y_attn = mla_branch(c_q, c_kv, k_r, q_norm[l], w_uq[l], kv_norm[l], w_ukv[l], w_attn_o[l])
        y_rnn = rglru_branch(x_r, y_r, conv_w[l], conv_b[l], w_rg_a[l], b_rg_a[l],
                             w_rg_x[l], b_rg_x[l], lru_lambda[l], w_rnn_o[l])
        h = h + (g_attn * y_attn + g_rnn * y_rnn) @ w_out[l]
        h = h + peer_ffn(rms_norm(h, ffn_norm[l]), w_peer_q[l], peer_subkeys[l], peer_u[l], peer_v[l])
        gate = jax.nn.sigmoid(rms_norm(h, ple_norm[l]) @ w_ple_gate[l])
        h = h + gate * (p[l] @ w_ple_proj[l])
    return rms_norm(h, final_norm)
```

```python
import functools
import math

import jax
import jax.numpy as jnp
from jax import lax
from jax.experimental import pallas as pl
from jax.experimental.pallas import tpu as pltpu

RMS_EPS = 1e-6
N_HEADS = 16
QK_NOPE_DIM = 128
QK_ROPE_DIM = 64
V_HEAD_DIM = 128
QK_HEAD_DIM = QK_NOPE_DIM + QK_ROPE_DIM
Q_LORA_RANK = 768
KV_LORA_RANK = 512
ROPE_THETA = 10000.0
RNN_BLOCKS = 16
RNN_BLOCK_DIM = 128
CONV_WIDTH = 4
LRU_C = 8.0
PEER_HEADS = 8
N_KEYS = 128
PEER_HALF = 128
PEER_TOPK = 16

LANES = 128
SUBLANES = 8
VMEM_LIMIT_BYTES = 56 * 1024 * 1024

F32 = jnp.float32
BF16 = jnp.bfloat16
NT_DIMS = (((1,), (1,)), ((), ()))


def _cparams(*sem):
    return pltpu.CompilerParams(dimension_semantics=sem, vmem_limit_bytes=VMEM_LIMIT_BYTES)


def _rms(x, g):
    ms = jnp.mean(x * x, axis=-1, keepdims=True)
    return (x * lax.rsqrt(ms + RMS_EPS)) * g


def _rmsnorm_kernel(x_ref, g_ref, o_ref):
    o_ref[...] = _rms(x_ref[...], g_ref[...]).astype(o_ref.dtype)


def rmsnorm_bf16(x, g, *, tr=512):
    m, d = x.shape
    tr = min(tr, m)
    return pl.pallas_call(
        _rmsnorm_kernel,
        grid=(m // tr,),
        in_specs=[pl.BlockSpec((tr, d), lambda i: (i, 0)), pl.BlockSpec((1, d), lambda i: (0, 0))],
        out_specs=pl.BlockSpec((tr, d), lambda i: (i, 0)),
        out_shape=jax.ShapeDtypeStruct((m, d), BF16),
        compiler_params=_cparams("parallel"),
        name="rmsnorm",
    )(x, g.reshape(1, d))


def _mm_kernel(a_ref, w_ref, *refs, n_extra, prologue, body):
    extra = refs[:n_extra]
    outs = refs[n_extra:]
    a = a_ref[...]
    if prologue is not None:
        a = prologue(a, extra)
    acc = jnp.dot(a, w_ref[...], preferred_element_type=F32)
    body(acc, extra, outs)


def _mm_call(name, a, w, *, tm, tn, extras=(), extra_specs=(), out_shapes, out_specs, body, prologue=None):
    m, k = a.shape
    n = w.shape[1]
    tm = min(tm, m)
    assert m % tm == 0 and n % tn == 0
    kern = functools.partial(_mm_kernel, n_extra=len(extras), prologue=prologue, body=body)
    return pl.pallas_call(
        kern,
        grid=(n // tn, m // tm),
        in_specs=[pl.BlockSpec((tm, k), lambda j, i: (i, 0)), pl.BlockSpec((k, tn), lambda j, i: (0, j))]
        + list(extra_specs),
        out_specs=out_specs,
        out_shape=out_shapes,
        compiler_params=_cparams("parallel", "parallel"),
        name=name,
    )(a, w, *extras)


def _tile_spec(tm, tn, col_off_blocks=0):
    return pl.BlockSpec((tm, tn), lambda j, i: (i, j + col_off_blocks))


def _col_spec(tn):
    return pl.BlockSpec((1, tn), lambda j, i: (0, j))


def _attn_kernel(qn_ref, qpe_ref, kv_ref, kpe_ref, o_ref, *, tq):
    i = pl.program_id(1)
    qpe = qpe_ref[...]
    row = lax.broadcasted_iota(jnp.int32, (tq, tq), 0)
    col = lax.broadcasted_iota(jnp.int32, (tq, tq), 1)
    for hh in range(2):
        q = jnp.concatenate([qn_ref[:, hh * LANES:(hh + 1) * LANES], qpe], axis=1)

        def step(kb, carry, masked, hh=hh, q=q):
            m, l, acc = carry
            start = pl.multiple_of(kb * tq, tq)
            k = jnp.concatenate(
                [kv_ref[pl.ds(start, tq), hh * 256:hh * 256 + LANES],
                 kpe_ref[pl.ds(start, tq), hh * LANES:(hh + 1) * LANES]], axis=1)
            v = kv_ref[pl.ds(start, tq), hh * 256 + LANES:(hh + 1) * 256]
            s = lax.dot_general(q, k, NT_DIMS, preferred_element_type=F32)
            if masked:
                s = jnp.where(col <= row, s, -jnp.inf)
            m_new = jnp.maximum(m, jnp.max(s, axis=-1, keepdims=True))
            alpha = jnp.exp(m - m_new)
            p = jnp.exp(s - m_new)
            l = alpha * l + jnp.sum(p, axis=-1, keepdims=True)
            acc = alpha * acc + jnp.dot(p.astype(BF16), v, preferred_element_type=F32)
            return m_new, l, acc

        init = (jnp.full((tq, 1), -jnp.inf, F32), jnp.zeros((tq, 1), F32), jnp.zeros((tq, V_HEAD_DIM), F32))
        carry = lax.fori_loop(0, i, lambda kb, c: step(kb, c, False), init)
        m, l, acc = step(i, carry, True)
        o_ref[:, hh * LANES:(hh + 1) * LANES] = (acc / l).astype(o_ref.dtype)


def mla_attention(qn, qpe, kv, kpe, *, tq=512):
    s = qn.shape[0]
    tq = min(tq, s)
    n_pairs = N_HEADS // 2
    return pl.pallas_call(
        functools.partial(_attn_kernel, tq=tq),
        grid=(n_pairs, s // tq),
        in_specs=[
            pl.BlockSpec((tq, 2 * LANES), lambda j, i: (i, j)),
            pl.BlockSpec((tq, LANES), lambda j, i: (i, j)),
            pl.BlockSpec((s, 4 * LANES), lambda j, i: (0, j)),
            pl.BlockSpec((s, 2 * LANES), lambda j, i: (0, 0)),
        ],
        out_specs=pl.BlockSpec((tq, 2 * LANES), lambda j, i: (i, j)),
        out_shape=jax.ShapeDtypeStruct((s, N_HEADS * V_HEAD_DIM), BF16),
        compiler_params=_cparams("parallel", "parallel"),
        name="mla_attention",
    )(qn, qpe, kv, kpe)


def _softplus(z):
    return jnp.maximum(z, 0.0) + jnp.log1p(jnp.exp(-jnp.abs(z)))


def _rglru_kernel(x_ref, gy_ref, cw_ref, cb_ref, wa_ref, ba_ref, wx_ref, bx_ref, lam_ref, y_ref,
                  h_sc, prev_sc, *, ts):
    tb = pl.program_id(1)

    @pl.when(tb == 0)
    def _():
        h_sc[...] = jnp.zeros_like(h_sc)
        prev_sc[...] = jnp.zeros_like(prev_sc)

    x = x_ref[...]
    prev = prev_sc[...]
    cw = cw_ref[...]
    row8 = lax.broadcasted_iota(jnp.int32, (SUBLANES, LANES), 0)
    xc = x * cw[CONV_WIDTH - 1:CONV_WIDTH, :] + cb_ref[...]
    for d in range(1, CONV_WIDTH):
        rolled = pltpu.roll(x, d, 0)
        fix = pltpu.roll(prev, d, 0)
        head = jnp.where(row8 < d, fix, rolled[:SUBLANES])
        xs = jnp.concatenate([head, rolled[SUBLANES:]], axis=0)
        xc = xc + xs * cw[CONV_WIDTH - 1 - d:CONV_WIDTH - d, :]
    prev_sc[...] = x[ts - SUBLANES:, :]

    xcb = xc.astype(BF16)
    r = jax.nn.sigmoid(jnp.dot(xcb, wa_ref[...], preferred_element_type=F32) + ba_ref[...])
    ig = jax.nn.sigmoid(jnp.dot(xcb, wx_ref[...], preferred_element_type=F32) + bx_ref[...])
    log_a = (-LRU_C) * r * _softplus(-lam_ref[...])
    a = jnp.exp(log_a)
    th = jnp.tanh(log_a)
    mult = jnp.sqrt(-2.0 * th / (1.0 - th))
    rowt = lax.broadcasted_iota(jnp.int32, (ts, LANES), 0)
    mult = jnp.where(rowt + tb * ts == 0, 1.0, mult)
    b = mult * (ig * xc)

    d = 1
    while d < ts:
        if d < SUBLANES:
            a_sh = pltpu.roll(a, d, 0)
            b_sh = pltpu.roll(b, d, 0)
            keep = rowt >= d
            b = jnp.where(keep, a * b_sh + b, b)
            a = jnp.where(keep, a * a_sh, a)
        else:
            a_sh = jnp.concatenate([jnp.ones((d, LANES), F32), a[:ts - d]], axis=0)
            b_sh = jnp.concatenate([jnp.zeros((d, LANES), F32), b[:ts - d]], axis=0)
            b = a * b_sh + b
            a = a * a_sh
        d *= 2
    h = a * h_sc[0:1, :] + b
    h_sc[...] = jnp.broadcast_to(h[ts - 1:ts, :], h_sc.shape)
    y_ref[...] = (h * gy_ref[...].astype(F32)).astype(y_ref.dtype)


def rglru(x_r, gy, conv_w, conv_b, w_rg_a, b_rg_a, w_rg_x, b_rg_x, lru_lambda, *, ts=512):
    s, c = x_r.shape
    ts = min(ts, s)
    nb = c // LANES
    col = lambda cb, tb: (0, cb)
    blk = lambda cb, tb: (cb, 0, 0)
    return pl.pallas_call(
        functools.partial(_rglru_kernel, ts=ts),
        grid=(nb, s // ts),
        in_specs=[
            pl.BlockSpec((ts, LANES), lambda cb, tb: (tb, cb)),
            pl.BlockSpec((ts, LANES), lambda cb, tb: (tb, cb)),
            pl.BlockSpec((CONV_WIDTH, LANES), col),
            pl.BlockSpec((1, LANES), col),
            pl.BlockSpec((None, RNN_BLOCK_DIM, RNN_BLOCK_DIM), blk),
            pl.BlockSpec((None, 1, RNN_BLOCK_DIM), blk),
            pl.BlockSpec((None, RNN_BLOCK_DIM, RNN_BLOCK_DIM), blk),
            pl.BlockSpec((None, 1, RNN_BLOCK_DIM), blk),
            pl.BlockSpec((1, LANES), col),
        ],
        out_specs=pl.BlockSpec((ts, LANES), lambda cb, tb: (tb, cb)),
        out_shape=jax.ShapeDtypeStruct((s, c), BF16),
        scratch_shapes=[pltpu.VMEM((SUBLANES, LANES), F32), pltpu.VMEM((SUBLANES, LANES), F32)],
        compiler_params=_cparams("parallel", "arbitrary"),
        name="rglru",
    )(x_r, gy, conv_w, conv_b.reshape(1, c), w_rg_a.astype(BF16), b_rg_a.reshape(nb, 1, RNN_BLOCK_DIM),
      w_rg_x.astype(BF16), b_rg_x.reshape(nb, 1, RNN_BLOCK_DIM), lru_lambda.reshape(1, c))


def _topk_rows(s, k, vals_ref, idx_ref):
    n_rows = s.shape[0]
    row = lax.broadcasted_iota(jnp.int32, s.shape, 0)
    for r in range(k):
        m = jnp.max(s, axis=0, keepdims=True)
        am = jnp.min(jnp.where(s == m, row, n_rows), axis=0, keepdims=True)
        vals_ref[r:r + 1, :] = m
        idx_ref[r:r + 1, :] = am
        s = jnp.where(row == am, -jnp.inf, s)


def _peer_route_kernel(pq_ref, sk_ref, i1_ref, i2_ref, g_ref, v0_sc, i0_sc, v1_sc, j1_sc, bs_sc, bc_sc):
    k = PEER_TOPK
    for c, (v_sc, i_sc) in enumerate(((v0_sc, i0_sc), (v1_sc, j1_sc))):
        s_t = lax.dot_general(sk_ref[c], pq_ref[c].astype(BF16), NT_DIMS, preferred_element_type=F32)
        _topk_rows(s_t, k, v_sc, i_sc)
    v1 = v1_sc[...]
    cand = jnp.concatenate([v0_sc[a:a + 1, :] + v1 for a in range(k)], axis=0)
    _topk_rows(cand, k, bs_sc, bc_sc)
    bs = bs_sc[...]
    bc = bc_sc[...]
    row = lax.broadcasted_iota(jnp.int32, bs.shape, 0)
    i0 = i0_sc[...]
    j1 = j1_sc[...]
    for r in range(k):
        ca = bc[r:r + 1, :] >> 4
        cb = bc[r:r + 1, :] & (k - 1)
        i1_ref[r:r + 1, :] = jnp.sum(jnp.where(row == ca, i0, 0), axis=0, keepdims=True)
        i2_ref[r:r + 1, :] = jnp.sum(jnp.where(row == cb, j1, 0), axis=0, keepdims=True)
    e = jnp.exp(bs - bs[0:1, :])
    g_ref[...] = e / jnp.sum(e, axis=0, keepdims=True)


def peer_route(pq, subkeys, *, tt=256):
    s = pq.shape[1]
    tt = min(tt, s)
    k = PEER_TOPK
    out = jax.ShapeDtypeStruct((PEER_HEADS * k, s), jnp.int32)
    ospec = pl.BlockSpec((k, tt), lambda i, h: (h, i))
    return pl.pallas_call(
        _peer_route_kernel,
        grid=(s // tt, PEER_HEADS),
        in_specs=[pl.BlockSpec((2, tt, PEER_HALF), lambda i, h: (h, i, 0)),
                  pl.BlockSpec((2, N_KEYS, PEER_HALF), lambda i, h: (h, 0, 0))],
        out_specs=[ospec, ospec, ospec],
        out_shape=[out, out, jax.ShapeDtypeStruct((PEER_HEADS * k, s), F32)],
        scratch_shapes=[pltpu.VMEM((k, tt), F32), pltpu.VMEM((k, tt), jnp.int32),
                        pltpu.VMEM((k, tt), F32), pltpu.VMEM((k, tt), jnp.int32),
                        pltpu.VMEM((k, tt), F32), pltpu.VMEM((k, tt), jnp.int32)],
        compiler_params=_cparams("parallel", "parallel"),
        name="peer_route",
    )(pq, subkeys)


def _peer_table_kernel(i1t_ref, i2t_ref, gt_ref, w_ref, i1_sc, i2_sc, g_sc, *, tw, unroll):
    i1_sc[...] = i1t_ref[...].T
    i2_sc[...] = i2t_ref[...].T
    g_sc[...] = gt_ref[...].T
    sub = lax.broadcasted_iota(jnp.int32, (N_KEYS, PEER_HEADS * PEER_TOPK), 0)

    def body(t, carry):
        r1 = i1_sc[pl.ds(t, 1), :]
        r2 = i2_sc[pl.ds(t, 1), :]
        gg = g_sc[pl.ds(t, 1), :]
        a = jnp.where(sub == r1, gg, 0.0).astype(BF16)
        b = jnp.where(sub == r2, 1.0, 0.0).astype(BF16)
        w_ref[t] = lax.dot_general(a, b, NT_DIMS, preferred_element_type=F32).astype(w_ref.dtype)
        return carry

    lax.fori_loop(0, tw, body, 0, unroll=unroll)


def peer_table(i1t, i2t, gt, *, tw=256, unroll=4):
    hk, s = i1t.shape
    tw = min(tw, s)
    ispec = pl.BlockSpec((hk, tw), lambda i: (0, i))
    return pl.pallas_call(
        functools.partial(_peer_table_kernel, tw=tw, unroll=unroll),
        grid=(s // tw,),
        in_specs=[ispec, ispec, ispec],
        out_specs=pl.BlockSpec((tw, N_KEYS, N_KEYS), lambda i: (i, 0, 0)),
        out_shape=jax.ShapeDtypeStruct((s, N_KEYS, N_KEYS), BF16),
        scratch_shapes=[pltpu.VMEM((tw, hk), jnp.int32), pltpu.VMEM((tw, hk), jnp.int32),
                        pltpu.VMEM((tw, hk), F32)],
        compiler_params=_cparams("parallel"),
        name="peer_table",
    )(i1t, i2t, gt)


def _peer_experts_kernel(n_ref, w_ref, u_ref, v_ref, h_ref, o_ref, acc_ref):
    j = pl.program_id(1)

    @pl.when(j == 0)
    def _():
        acc_ref[...] = jnp.zeros_like(acc_ref)

    act = lax.dot_general(n_ref[...], u_ref[...], NT_DIMS, preferred_element_type=F32)
    m = (jax.nn.gelu(act) * w_ref[...].astype(F32)).astype(BF16)
    acc_ref[...] += jnp.dot(m, v_ref[...], preferred_element_type=F32)

    @pl.when(j == pl.num_programs(1) - 1)
    def _():
        o_ref[...] = h_ref[...] + acc_ref[...]


def peer_experts(n2, w_flat, u, v, h1, *, tm=512, te=512):
    s, d = n2.shape
    e = u.shape[0]
    tm = min(tm, s)
    return pl.pallas_call(
        _peer_experts_kernel,
        grid=(s // tm, e // te),
        in_specs=[
            pl.BlockSpec((tm, d), lambda i, j: (i, 0)),
            pl.BlockSpec((tm, te), lambda i, j: (i, j)),
            pl.BlockSpec((te, d), lambda i, j: (j, 0)),
            pl.BlockSpec((te, d), lambda i, j: (j, 0)),
            pl.BlockSpec((tm, d), lambda i, j: (i, 0)),
        ],
        out_specs=pl.BlockSpec((tm, d), lambda i, j: (i, 0)),
        out_shape=jax.ShapeDtypeStruct((s, d), F32),
        scratch_shapes=[pltpu.VMEM((tm, d), F32)],
        compiler_params=_cparams("parallel", "arbitrary"),
        name="peer_experts",
    )(n2, w_flat, u, v, h1)


def _ple_kernel(h_ref, p_ref, gple_ref, wg_ref, wp_ref, gfin_ref, o_ref, *, final):
    h = h_ref[...]
    n3 = _rms(h, gple_ref[...]).astype(BF16)
    gate = jax.nn.sigmoid(jnp.dot(n3, wg_ref[...], preferred_element_type=F32))
    pp = jnp.dot(p_ref[...], wp_ref[...], preferred_element_type=F32)
    h = h + gate * pp
    if final:
        h = _rms(h, gfin_ref[...])
    o_ref[...] = h


def ple_block(h2, p, ple_norm, w_gate, w_proj, final_norm, *, final, tm=256):
    s, d = h2.shape
    pd = p.shape[1]
    tm = min(tm, s)
    full = lambda i: (0, 0)
    return pl.pallas_call(
        functools.partial(_ple_kernel, final=final),
        grid=(s // tm,),
        in_specs=[
            pl.BlockSpec((tm, d), lambda i: (i, 0)),
            pl.BlockSpec((tm, pd), lambda i: (i, 0)),
            pl.BlockSpec((1, d), full),
            pl.BlockSpec((d, d), full),
            pl.BlockSpec((pd, d), full),
            pl.BlockSpec((1, d), full),
        ],
        out_specs=pl.BlockSpec((tm, d), lambda i: (i, 0)),
        out_shape=jax.ShapeDtypeStruct((s, d), F32),
        compiler_params=_cparams("parallel"),
        name="ple_block",
    )(h2, p, ple_norm.reshape(1, d), w_gate, w_proj, final_norm.reshape(1, d))


def _rot_cols(w):
    half = QK_ROPE_DIM // 2
    return jnp.concatenate([-w[..., half:], w[..., :half]], axis=-1)


def _layer(h, p, cos128, sin128, prm, *, final, final_norm):
    s, d = h.shape
    (attn_norm, w_in, b_gate, q_norm, w_uq, kv_norm, w_ukv, w_attn_o, conv_w, conv_b, w_rg_a, b_rg_a,
     w_rg_x, b_rg_x, lru_lambda, w_rnn_o, w_out, ffn_norm, w_peer_q, peer_subkeys, peer_u, peer_v,
     ple_norm, w_ple_gate, w_ple_proj) = prm

    o_q, o_kv, o_kr, o_x, o_y, o_g = 0, Q_LORA_RANK, Q_LORA_RANK + KV_LORA_RANK, \
        Q_LORA_RANK + KV_LORA_RANK + QK_ROPE_DIM, Q_LORA_RANK + KV_LORA_RANK + QK_ROPE_DIM + d, \
        Q_LORA_RANK + KV_LORA_RANK + QK_ROPE_DIM + 2 * d
    w_kr = w_in[:, o_kr:o_x]
    w_kr_rot = _rot_cols(w_kr)
    w_lat = jnp.concatenate([w_in[:, o_q:o_kr], w_kr, w_kr, w_kr_rot, w_kr_rot], axis=1).astype(BF16)
    w_x = w_in[:, o_x:o_y].astype(BF16)
    w_y = w_in[:, o_y:o_g].astype(BF16)
    w_g = w_in[:, o_g:].astype(BF16)
    uq = w_uq.reshape(Q_LORA_RANK, N_HEADS, QK_HEAD_DIM)
    uq_pe = uq[:, :, QK_NOPE_DIM:]
    w_q = jnp.concatenate([uq[:, :, :QK_NOPE_DIM].reshape(Q_LORA_RANK, -1),
                           uq_pe.reshape(Q_LORA_RANK, -1),
                           _rot_cols(uq_pe).reshape(Q_LORA_RANK, -1)], axis=1).astype(BF16)
    n_lat = Q_LORA_RANK + KV_LORA_RANK + 2 * LANES
    n_qn = N_HEADS * QK_NOPE_DIM
    n_qp = N_HEADS * QK_ROPE_DIM
    scale = QK_HEAD_DIM ** -0.5

    n1 = rmsnorm_bf16(h, attn_norm)

    def lat_body(acc, extra, outs):
        outs[0][...] = acc[:, :Q_LORA_RANK]
        outs[1][...] = acc[:, Q_LORA_RANK:Q_LORA_RANK + KV_LORA_RANK]
        outs[2][...] = acc[:, Q_LORA_RANK + KV_LORA_RANK:]

    tm_lat = 512
    c_q, c_kv, kr4 = _mm_call(
        "in_proj_latents", n1, w_lat, tm=tm_lat, tn=n_lat, body=lat_body,
        out_shapes=[jax.ShapeDtypeStruct((s, Q_LORA_RANK), F32), jax.ShapeDtypeStruct((s, KV_LORA_RANK), F32),
                    jax.ShapeDtypeStruct((s, 2 * LANES), F32)],
        out_specs=[pl.BlockSpec((min(tm_lat, s), Q_LORA_RANK), lambda j, i: (i, 0)),
                   pl.BlockSpec((min(tm_lat, s), KV_LORA_RANK), lambda j, i: (i, 0)),
                   pl.BlockSpec((min(tm_lat, s), 2 * LANES), lambda j, i: (i, 0))])

    tm, tn = min(512, s), 1024

    def store_body(acc, extra, outs):
        outs[0][...] = acc.astype(outs[0].dtype)

    x_r = _mm_call("in_proj_x", n1, w_x, tm=tm, tn=tn, body=store_body,
                   out_shapes=jax.ShapeDtypeStruct((s, d), F32), out_specs=_tile_spec(tm, tn))

    def gelu_body(acc, extra, outs):
        outs[0][...] = jax.nn.gelu(acc).astype(outs[0].dtype)

    gy = _mm_call("in_proj_y", n1, w_y, tm=tm, tn=tn, body=gelu_body,
                  out_shapes=jax.ShapeDtypeStruct((s, d), BF16), out_specs=_tile_spec(tm, tn))

    def gate_body(acc, extra, outs):
        outs[0][...] = jax.nn.sigmoid(acc + extra[0][...]).astype(outs[0].dtype)

    gates = _mm_call("in_proj_gates", n1, w_g, tm=tm, tn=tn, body=gate_body,
                     extras=(b_gate.reshape(1, 2 * d),), extra_specs=(_col_spec(tn),),
                     out_shapes=jax.ShapeDtypeStruct((s, 2 * d), BF16), out_specs=_tile_spec(tm, tn))

    def norm_prologue(a, extra):
        return _rms(a, extra[0][...]).astype(BF16)

    def q_body(acc, extra, outs):
        outs[0][...] = (acc[:, :n_qn] * scale).astype(BF16)
        cos_t = jnp.tile(extra[1][...], (1, n_qp // LANES))
        sin_t = jnp.tile(extra[2][...], (1, n_qp // LANES))
        pe = acc[:, n_qn:n_qn + n_qp] * cos_t + acc[:, n_qn + n_qp:] * sin_t
        outs[1][...] = (pe * scale).astype(BF16)

    tm_q = min(256, s)
    row128 = pl.BlockSpec((tm_q, LANES), lambda j, i: (i, 0))
    qn, qpe = _mm_call(
        "q_proj", c_q, w_q, tm=tm_q, tn=n_qn + 2 * n_qp, prologue=norm_prologue, body=q_body,
        extras=(q_norm.reshape(1, Q_LORA_RANK), cos128, sin128),
        extra_specs=(pl.BlockSpec((1, Q_LORA_RANK), lambda j, i: (0, 0)), row128, row128),
        out_shapes=[jax.ShapeDtypeStruct((s, n_qn), BF16), jax.ShapeDtypeStruct((s, n_qp), BF16)],
        out_specs=[pl.BlockSpec((tm_q, n_qn), lambda j, i: (i, 0)), pl.BlockSpec((tm_q, n_qp), lambda j, i: (i, 0))])

    def kv_body(acc, extra, outs):
        outs[0][...] = acc.astype(BF16)
        kr = extra[1][...]
        kpe2 = kr[:, :LANES] * extra[2][...] + kr[:, LANES:] * extra[3][...]
        lane = lax.broadcasted_iota(jnp.int32, kpe2.shape, 1)
        outs[1][...] = jnp.concatenate([jnp.where(lane < QK_ROPE_DIM, kpe2, 0.0),
                                        jnp.where(lane >= QK_ROPE_DIM, kpe2, 0.0)], axis=1).astype(BF16)

    n_kv = N_HEADS * (QK_NOPE_DIM + V_HEAD_DIM)
    kv, kpe = _mm_call(
        "kv_proj", c_kv, w_ukv.astype(BF16), tm=tm_q, tn=n_kv, prologue=norm_prologue, body=kv_body,
        extras=(kv_norm.reshape(1, KV_LORA_RANK), kr4, cos128, sin128),
        extra_specs=(pl.BlockSpec((1, KV_LORA_RANK), lambda j, i: (0, 0)),
                     pl.BlockSpec((tm_q, 2 * LANES), lambda j, i: (i, 0)), row128, row128),
        out_shapes=[jax.ShapeDtypeStruct((s, n_kv), BF16), jax.ShapeDtypeStruct((s, 2 * LANES), BF16)],
        out_specs=[pl.BlockSpec((tm_q, n_kv), lambda j, i: (i, 0)), pl.BlockSpec((tm_q, 2 * LANES), lambda j, i: (i, 0))])

    o_attn = mla_attention(qn, qpe, kv, kpe)
    y_rnn_in = rglru(x_r, gy, conv_w, conv_b, w_rg_a, b_rg_a, w_rg_x, b_rg_x, lru_lambda)

    def merge_kernel(o_ref, y_ref, wo_ref, wr_ref, ga_ref, gr_ref, m_ref):
        ya = jnp.dot(o_ref[...], wo_ref[...], preferred_element_type=F32)
        yr = jnp.dot(y_ref[...], wr_ref[...], preferred_element_type=F32)
        m_ref[...] = (ga_ref[...].astype(F32) * ya + gr_ref[...].astype(F32) * yr).astype(m_ref.dtype)

    a_spec = pl.BlockSpec((tm, d), lambda j, i: (i, 0))
    w_spec = pl.BlockSpec((d, tn), lambda j, i: (0, j))
    merged = pl.pallas_call(
        merge_kernel,
        grid=(d // tn, s // tm),
        in_specs=[a_spec, a_spec, w_spec, w_spec, _tile_spec(tm, tn), _tile_spec(tm, tn, d // tn)],
        out_specs=_tile_spec(tm, tn),
        out_shape=jax.ShapeDtypeStruct((s, d), BF16),
        compiler_params=_cparams("parallel", "parallel"),
        name="mixer_merge",
    )(o_attn, y_rnn_in, w_attn_o.astype(BF16), w_rnn_o.astype(BF16), gates, gates)

    def out_body(acc, extra, outs):
        h1 = extra[0][...] + acc
        outs[0][...] = h1
        outs[1][...] = _rms(h1, extra[1][...]).astype(BF16)

    tm_o = min(256, s)
    rowd = pl.BlockSpec((tm_o, d), lambda j, i: (i, 0))
    h1, n2 = _mm_call(
        "out_proj", merged, w_out.astype(BF16), tm=tm_o, tn=d, body=out_body,
        extras=(h, ffn_norm.reshape(1, d)), extra_specs=(rowd, pl.BlockSpec((1, d), lambda j, i: (0, 0))),
        out_shapes=[jax.ShapeDtypeStruct((s, d), F32), jax.ShapeDtypeStruct((s, d), BF16)],
        out_specs=[rowd, rowd])

    n_hc = 2 * PEER_HEADS

    def pq_body(acc, extra, outs):
        for c in range(n_hc):
            outs[0][c] = acc[:, c * PEER_HALF:(c + 1) * PEER_HALF]

    pq = _mm_call("peer_query", n2, w_peer_q.astype(BF16), tm=tm_o, tn=n_hc * PEER_HALF, body=pq_body,
                  out_shapes=jax.ShapeDtypeStruct((n_hc, s, PEER_HALF), F32),
                  out_specs=pl.BlockSpec((n_hc, tm_o, PEER_HALF), lambda j, i: (0, i, 0)))
    i1t, i2t, gt = peer_route(pq, peer_subkeys.reshape(n_hc, N_KEYS, PEER_HALF).astype(BF16))
    w_tab = peer_table(i1t, i2t, gt)
    h2 = peer_experts(n2, w_tab.reshape(s, N_KEYS * N_KEYS), peer_u.astype(BF16), peer_v.astype(BF16), h1)

    return ple_block(h2, p.astype(BF16), ple_norm, w_ple_gate.astype(BF16), w_ple_proj.astype(BF16),
                     final_norm, final=final)


def kernel(x, p, attn_norm, w_in, b_gate, q_norm, w_uq, kv_norm, w_ukv, w_attn_o, conv_w, conv_b, w_rg_a, b_rg_a,
           w_rg_x, b_rg_x, lru_lambda, w_rnn_o, w_out, ffn_norm, w_peer_q, peer_subkeys, peer_u, peer_v, ple_norm,
           w_ple_gate, w_ple_proj, final_norm):
    bsz, s, d = x.shape
    assert bsz == 1, "one sequence per call"
    depth = w_in.shape[0]
    layer_params = (attn_norm, w_in, b_gate, q_norm, w_uq, kv_norm, w_ukv, w_attn_o, conv_w, conv_b, w_rg_a,
                    b_rg_a, w_rg_x, b_rg_x, lru_lambda, w_rnn_o, w_out, ffn_norm, w_peer_q, peer_subkeys, peer_u,
                    peer_v, ple_norm, w_ple_gate, w_ple_proj)

    half = QK_ROPE_DIM // 2
    inv_freq = 1.0 / (ROPE_THETA ** (jnp.arange(half, dtype=F32) / half))
    ang = jnp.arange(s, dtype=F32)[:, None] * inv_freq[None, :]
    cos128 = jnp.tile(jnp.cos(ang), (1, LANES // half))
    sin128 = jnp.tile(jnp.sin(ang), (1, LANES // half))

    h = x.reshape(s, d)
    for l in range(depth):
        h = _layer(h, p[l, 0], cos128, sin128, tuple(w[l] for w in layer_params),
                   final=(l == depth - 1), final_norm=final_norm)
    return h.reshape(bsz, s, d)
```

```python
import functools
import math

import jax
import jax.numpy as jnp
from jax import lax
from jax.experimental import pallas as pl
from jax.experimental.pallas import tpu as pltpu

RMS_EPS = 1e-6
N_HEADS = 16
QK_NOPE_DIM = 128
QK_ROPE_DIM = 64
V_HEAD_DIM = 128
QK_HEAD_DIM = QK_NOPE_DIM + QK_ROPE_DIM
Q_LORA_RANK = 768
KV_LORA_RANK = 512
ROPE_THETA = 10000.0
RNN_BLOCKS = 16
RNN_BLOCK_DIM = 128
CONV_WIDTH = 4
LRU_C = 8.0
PEER_HEADS = 8
N_KEYS = 128
PEER_HALF = 128
PEER_TOPK = 16

LANES = 128
SUBLANES = 8
VMEM_LIMIT_BYTES = 56 * 1024 * 1024

F32 = jnp.float32
BF16 = jnp.bfloat16
NT_DIMS = (((1,), (1,)), ((), ()))


def _cparams(*sem):
    return pltpu.CompilerParams(dimension_semantics=sem, vmem_limit_bytes=VMEM_LIMIT_BYTES)


def _rms(x, g):
    ms = jnp.mean(x * x, axis=-1, keepdims=True)
    return (x * lax.rsqrt(ms + RMS_EPS)) * g


def _rmsnorm_kernel(x_ref, g_ref, o_ref):
    o_ref[...] = _rms(x_ref[...], g_ref[...]).astype(o_ref.dtype)


def rmsnorm_bf16(x, g, *, tr=512):
    m, d = x.shape
    tr = min(tr, m)
    return pl.pallas_call(
        _rmsnorm_kernel,
        grid=(m // tr,),
        in_specs=[pl.BlockSpec((tr, d), lambda i: (i, 0)), pl.BlockSpec((1, d), lambda i: (0, 0))],
        out_specs=pl.BlockSpec((tr, d), lambda i: (i, 0)),
        out_shape=jax.ShapeDtypeStruct((m, d), BF16),
        compiler_params=_cparams("parallel"),
        name="rmsnorm",
    )(x, g.reshape(1, d))


def _mm_kernel(a_ref, w_ref, *refs, n_extra, prologue, body):
    extra = refs[:n_extra]
    outs = refs[n_extra:]
    a = a_ref[...]
    if prologue is not None:
        a = prologue(a, extra)
    acc = jnp.dot(a, w_ref[...], preferred_element_type=F32)
    body(acc, extra, outs)


def _mm_call(name, a, w, *, tm, tn, extras=(), extra_specs=(), out_shapes, out_specs, body, prologue=None):
    m, k = a.shape
    n = w.shape[1]
    tm = min(tm, m)
    assert m % tm == 0 and n % tn == 0
    kern = functools.partial(_mm_kernel, n_extra=len(extras), prologue=prologue, body=body)
    return pl.pallas_call(
        kern,
        grid=(n // tn, m // tm),
        in_specs=[pl.BlockSpec((tm, k), lambda j, i: (i, 0)), pl.BlockSpec((k, tn), lambda j, i: (0, j))]
        + list(extra_specs),
        out_specs=out_specs,
        out_shape=out_shapes,
        compiler_params=_cparams("parallel", "parallel"),
        name=name,
    )(a, w, *extras)


def _tile_spec(tm, tn, col_off_blocks=0):
    return pl.BlockSpec((tm, tn), lambda j, i: (i, j + col_off_blocks))


def _col_spec(tn):
    return pl.BlockSpec((1, tn), lambda j, i: (0, j))


ATTN_ROW_CHUNK = 32


def _attn_kernel(qn_ref, qpe_ref, kv_ref, kpe_ref, o_ref, s_sc, p_sc, m_sc, l_sc, a_sc, acc_sc, *, tq):
    i = pl.program_id(1)
    rc = ATTN_ROW_CHUNK
    qpe = qpe_ref[...]
    qs = [jnp.concatenate([qn_ref[:, hh * LANES:(hh + 1) * LANES], qpe], axis=1) for hh in range(2)]
    m_sc[...] = jnp.full(m_sc.shape, -jnp.inf, F32)
    l_sc[...] = jnp.zeros(l_sc.shape, F32)
    acc_sc[...] = jnp.zeros(acc_sc.shape, F32)

    def step(kb, masked):
        start = pl.multiple_of(kb * tq, tq)
        for hh in range(2):
            k = jnp.concatenate(
                [kv_ref[pl.ds(start, tq), hh * 256:hh * 256 + LANES],
                 kpe_ref[pl.ds(start, tq), hh * LANES:(hh + 1) * LANES]], axis=1)
            s = lax.dot_general(qs[hh], k, NT_DIMS, preferred_element_type=F32)
            if masked:
                row = lax.broadcasted_iota(jnp.int32, (tq, tq), 0)
                col = lax.broadcasted_iota(jnp.int32, (tq, tq), 1)
                s = jnp.where(col <= row, s, -jnp.inf)
            s_sc[hh] = s
        chunks = [slice(c * rc, (c + 1) * rc) for c in range(tq // rc)]
        for hh in range(2):
            for rows in chunks:
                m_old = m_sc[hh, rows, :]
                m_new = jnp.maximum(m_old, jnp.max(s_sc[hh, rows, :], axis=-1, keepdims=True))
                a_sc[hh, rows, :] = jnp.exp2(m_old - m_new)
                m_sc[hh, rows, :] = m_new
            for rows in chunks:
                m_new = jnp.concatenate([m_sc[hh, rows, :]] * (tq // LANES), axis=1)
                p = jnp.exp2(s_sc[hh, rows, :] - m_new)
                alpha = a_sc[hh, rows, :]
                l_sc[hh, rows, :] = alpha * l_sc[hh, rows, :] + jnp.sum(p, axis=-1, keepdims=True)
                p_sc[hh, rows, :] = p.astype(BF16)
                acc_sc[hh, rows, :] = alpha * acc_sc[hh, rows, :]
        for hh in range(2):
            v = kv_ref[pl.ds(start, tq), hh * 256 + LANES:(hh + 1) * 256]
            acc_sc[hh] += jnp.dot(p_sc[hh], v, preferred_element_type=F32)

    def body(kb, carry):
        step(kb, False)
        return carry

    lax.fori_loop(0, i, body, 0)
    step(i, True)
    for hh in range(2):
        o_ref[:, hh * LANES:(hh + 1) * LANES] = (acc_sc[hh] / l_sc[hh]).astype(o_ref.dtype)


def mla_attention(qn, qpe, kv, kpe, *, tq=512):
    s = qn.shape[0]
    tq = min(tq, s)
    n_pairs = N_HEADS // 2
    return pl.pallas_call(
        functools.partial(_attn_kernel, tq=tq),
        grid=(n_pairs, s // tq),
        in_specs=[
            pl.BlockSpec((tq, 2 * LANES), lambda j, i: (i, j)),
            pl.BlockSpec((tq, LANES), lambda j, i: (i, j)),
            pl.BlockSpec((s, 4 * LANES), lambda j, i: (0, j)),
            pl.BlockSpec((s, 2 * LANES), lambda j, i: (0, 0)),
        ],
        out_specs=pl.BlockSpec((tq, 2 * LANES), lambda j, i: (i, j)),
        out_shape=jax.ShapeDtypeStruct((s, N_HEADS * V_HEAD_DIM), BF16),
        scratch_shapes=[pltpu.VMEM((2, tq, tq), F32), pltpu.VMEM((2, tq, tq), BF16),
                        pltpu.VMEM((2, tq, LANES), F32), pltpu.VMEM((2, tq, LANES), F32),
                        pltpu.VMEM((2, tq, LANES), F32), pltpu.VMEM((2, tq, V_HEAD_DIM), F32)],
        compiler_params=_cparams("parallel", "parallel"),
        name="mla_attention",
    )(qn, qpe, kv, kpe)


def _softplus(z):
    return jnp.maximum(z, 0.0) + jnp.log1p(jnp.exp(-jnp.abs(z)))


def _rglru_kernel(x_ref, gy_ref, cw_ref, cb_ref, wa_ref, ba_ref, wx_ref, bx_ref, lam_ref, y_ref,
                  h_sc, prev_sc, *, ts):
    tb = pl.program_id(1)

    @pl.when(tb == 0)
    def _():
        h_sc[...] = jnp.zeros_like(h_sc)
        prev_sc[...] = jnp.zeros_like(prev_sc)

    x = x_ref[...]
    prev = prev_sc[...]
    cw = cw_ref[...]
    row8 = lax.broadcasted_iota(jnp.int32, (SUBLANES, LANES), 0)
    xc = x * cw[CONV_WIDTH - 1:CONV_WIDTH, :] + cb_ref[...]
    for d in range(1, CONV_WIDTH):
        rolled = pltpu.roll(x, d, 0)
        fix = pltpu.roll(prev, d, 0)
        head = jnp.where(row8 < d, fix, rolled[:SUBLANES])
        xs = jnp.concatenate([head, rolled[SUBLANES:]], axis=0)
        xc = xc + xs * cw[CONV_WIDTH - 1 - d:CONV_WIDTH - d, :]
    prev_sc[...] = x[ts - SUBLANES:, :]

    xcb = xc.astype(BF16)
    r = jax.nn.sigmoid(jnp.dot(xcb, wa_ref[...], preferred_element_type=F32) + ba_ref[...])
    ig = jax.nn.sigmoid(jnp.dot(xcb, wx_ref[...], preferred_element_type=F32) + bx_ref[...])
    log_a = (-LRU_C) * r * _softplus(-lam_ref[...])
    a = jnp.exp(log_a)
    th = jnp.tanh(log_a)
    mult = jnp.sqrt(-2.0 * th / (1.0 - th))
    rowt = lax.broadcasted_iota(jnp.int32, (ts, LANES), 0)
    mult = jnp.where(rowt + tb * ts == 0, 1.0, mult)
    b = mult * (ig * xc)

    d = 1
    while d < ts:
        if d < SUBLANES:
            a_sh = pltpu.roll(a, d, 0)
            b_sh = pltpu.roll(b, d, 0)
            keep = rowt >= d
            b = jnp.where(keep, a * b_sh + b, b)
            a = jnp.where(keep, a * a_sh, a)
        else:
            a_sh = jnp.concatenate([jnp.ones((d, LANES), F32), a[:ts - d]], axis=0)
            b_sh = jnp.concatenate([jnp.zeros((d, LANES), F32), b[:ts - d]], axis=0)
            b = a * b_sh + b
            a = a * a_sh
        d *= 2
    h = a * h_sc[0:1, :] + b
    h_sc[...] = jnp.broadcast_to(h[ts - 1:ts, :], h_sc.shape)
    y_ref[...] = (h * gy_ref[...].astype(F32)).astype(y_ref.dtype)


def rglru(x_r, gy, conv_w, conv_b, w_rg_a, b_rg_a, w_rg_x, b_rg_x, lru_lambda, *, ts=512):
    s, c = x_r.shape
    ts = min(ts, s)
    nb = c // LANES
    col = lambda cb, tb: (0, cb)
    blk = lambda cb, tb: (cb, 0, 0)
    return pl.pallas_call(
        functools.partial(_rglru_kernel, ts=ts),
        grid=(nb, s // ts),
        in_specs=[
            pl.BlockSpec((ts, LANES), lambda cb, tb: (tb, cb)),
            pl.BlockSpec((ts, LANES), lambda cb, tb: (tb, cb)),
            pl.BlockSpec((CONV_WIDTH, LANES), col),
            pl.BlockSpec((1, LANES), col),
            pl.BlockSpec((None, RNN_BLOCK_DIM, RNN_BLOCK_DIM), blk),
            pl.BlockSpec((None, 1, RNN_BLOCK_DIM), blk),
            pl.BlockSpec((None, RNN_BLOCK_DIM, RNN_BLOCK_DIM), blk),
            pl.BlockSpec((None, 1, RNN_BLOCK_DIM), blk),
            pl.BlockSpec((1, LANES), col),
        ],
        out_specs=pl.BlockSpec((ts, LANES), lambda cb, tb: (tb, cb)),
        out_shape=jax.ShapeDtypeStruct((s, c), BF16),
        scratch_shapes=[pltpu.VMEM((SUBLANES, LANES), F32), pltpu.VMEM((SUBLANES, LANES), F32)],
        compiler_params=_cparams("parallel", "arbitrary"),
        name="rglru",
    )(x_r, gy, conv_w, conv_b.reshape(1, c), w_rg_a.astype(BF16), b_rg_a.reshape(nb, 1, RNN_BLOCK_DIM),
      w_rg_x.astype(BF16), b_rg_x.reshape(nb, 1, RNN_BLOCK_DIM), lru_lambda.reshape(1, c))


def _topk_rows(s, k, vals_ref, idx_ref):
    n_rows = s.shape[0]
    row = lax.broadcasted_iota(jnp.int32, s.shape, 0)
    for r in range(k):
        m = jnp.max(s, axis=0, keepdims=True)
        am = jnp.min(jnp.where(s == m, row, n_rows), axis=0, keepdims=True)
        vals_ref[r:r + 1, :] = m
        idx_ref[r:r + 1, :] = am
        s = jnp.where(row == am, -jnp.inf, s)


def _pair_candidates(v0_sc, v1_sc):
    k = PEER_TOPK
    tt = v0_sc.shape[1]
    row8 = lax.broadcasted_iota(jnp.int32, (SUBLANES, tt), 0)
    v0_lo = v0_sc[0:SUBLANES, :]
    vals = [v0_lo + v1_sc[0:1, :], v0_sc[SUBLANES:k, :] + v1_sc[0:1, :]]
    flat = [row8 * k, (row8 + SUBLANES) * k]
    for b in range(1, SUBLANES):
        ok = row8 < k // (b + 1)
        vals.append(jnp.where(ok, v0_lo + v1_sc[b:b + 1, :], -jnp.inf))
        flat.append(jnp.where(ok, row8 * k + b, k * k))
    vals.append(v0_sc[0:1, :] + v1_sc[SUBLANES:k, :])
    flat.append(row8 + SUBLANES)
    return jnp.concatenate(vals, axis=0), jnp.concatenate(flat, axis=0)


def _peer_route_kernel(pq_ref, sk_ref, i1_ref, i2_ref, g_ref, v0_sc, i0_sc, v1_sc, j1_sc, bs_sc, bc_sc):
    k = PEER_TOPK
    for c, (v_sc, i_sc) in enumerate(((v0_sc, i0_sc), (v1_sc, j1_sc))):
        s_t = lax.dot_general(sk_ref[c], pq_ref[c].astype(BF16), NT_DIMS, preferred_element_type=F32)
        _topk_rows(s_t, k, v_sc, i_sc)
    cand, flat = _pair_candidates(v0_sc, v1_sc)
    for r in range(k):
        m = jnp.max(cand, axis=0, keepdims=True)
        am = jnp.min(jnp.where(cand == m, flat, k * k), axis=0, keepdims=True)
        bs_sc[r:r + 1, :] = m
        bc_sc[r:r + 1, :] = am
        cand = jnp.where(flat == am, -jnp.inf, cand)
    bs = bs_sc[...]
    bc = bc_sc[...]
    row = lax.broadcasted_iota(jnp.int32, bs.shape, 0)
    i0 = i0_sc[...]
    j1 = j1_sc[...]
    for r in range(k):
        ca = bc[r:r + 1, :] >> 4
        cb = bc[r:r + 1, :] & (k - 1)
        i1_ref[r:r + 1, :] = jnp.sum(jnp.where(row == ca, i0, 0), axis=0, keepdims=True)
        i2_ref[r:r + 1, :] = jnp.sum(jnp.where(row == cb, j1, 0), axis=0, keepdims=True)
    e = jnp.exp(bs - bs[0:1, :])
    g_ref[...] = e / jnp.sum(e, axis=0, keepdims=True)


def peer_route(pq, subkeys, *, tt=256):
    s = pq.shape[1]
    tt = min(tt, s)
    k = PEER_TOPK
    out = jax.ShapeDtypeStruct((PEER_HEADS * k, s), jnp.int32)
    ospec = pl.BlockSpec((k, tt), lambda i, h: (h, i))
    return pl.pallas_call(
        _peer_route_kernel,
        grid=(s // tt, PEER_HEADS),
        in_specs=[pl.BlockSpec((2, tt, PEER_HALF), lambda i, h: (h, i, 0)),
                  pl.BlockSpec((2, N_KEYS, PEER_HALF), lambda i, h: (h, 0, 0))],
        out_specs=[ospec, ospec, ospec],
        out_shape=[out, out, jax.ShapeDtypeStruct((PEER_HEADS * k, s), F32)],
        scratch_shapes=[pltpu.VMEM((k, tt), F32), pltpu.VMEM((k, tt), jnp.int32),
                        pltpu.VMEM((k, tt), F32), pltpu.VMEM((k, tt), jnp.int32),
                        pltpu.VMEM((k, tt), F32), pltpu.VMEM((k, tt), jnp.int32)],
        compiler_params=_cparams("parallel", "parallel"),
        name="peer_route",
    )(pq, subkeys)


TABLE_GROUP = 16
TABLE_PITCH = N_KEYS + SUBLANES
TABLE_GROUPS_PER_STEP = 2


def _peer_table_kernel(i1t_ref, i2t_ref, gt_ref, w_ref, i1_sc, i2_sc, g_sc, stage_sc, *, tw):
    i1_sc[...] = i1t_ref[...].T
    i2_sc[...] = i2t_ref[...].T
    g_sc[...] = gt_ref[...].T
    sub = lax.broadcasted_iota(jnp.int32, (N_KEYS, PEER_HEADS * PEER_TOPK), 0)

    def groups(gi, carry):
        for sg in range(TABLE_GROUPS_PER_STEP):
            t0 = pl.multiple_of((gi * TABLE_GROUPS_PER_STEP + sg) * TABLE_GROUP, TABLE_GROUP)
            base = sg * TABLE_GROUP * TABLE_PITCH
            for u in range(TABLE_GROUP):
                r1 = i1_sc[pl.ds(t0 + u, 1), :]
                r2 = i2_sc[pl.ds(t0 + u, 1), :]
                gg = g_sc[pl.ds(t0 + u, 1), :]
                a = jnp.where(sub == r1, gg, 0.0).astype(BF16)
                b = jnp.where(sub == r2, 1.0, 0.0).astype(BF16)
                row0 = base + u * TABLE_PITCH
                stage_sc[row0:row0 + N_KEYS, :] = lax.dot_general(
                    a, b, NT_DIMS, preferred_element_type=F32)
            for k1 in range(N_KEYS):
                blk = stage_sc[pl.ds(base + k1, TABLE_GROUP, stride=TABLE_PITCH), :]
                w_ref[pl.ds(t0, TABLE_GROUP), k1 * N_KEYS:(k1 + 1) * N_KEYS] = blk.astype(w_ref.dtype)
        return carry

    lax.fori_loop(0, tw // (TABLE_GROUP * TABLE_GROUPS_PER_STEP), groups, 0)


def peer_table(i1t, i2t, gt, *, tw=256):
    hk, s = i1t.shape
    tw = min(tw, s)
    assert tw % (TABLE_GROUP * TABLE_GROUPS_PER_STEP) == 0
    ispec = pl.BlockSpec((hk, tw), lambda i: (0, i))
    return pl.pallas_call(
        functools.partial(_peer_table_kernel, tw=tw),
        grid=(s // tw,),
        in_specs=[ispec, ispec, ispec],
        out_specs=pl.BlockSpec((tw, N_KEYS * N_KEYS), lambda i: (i, 0)),
        out_shape=jax.ShapeDtypeStruct((s, N_KEYS * N_KEYS), BF16),
        scratch_shapes=[pltpu.VMEM((tw, hk), jnp.int32), pltpu.VMEM((tw, hk), jnp.int32),
                        pltpu.VMEM((tw, hk), F32),
                        pltpu.VMEM((TABLE_GROUPS_PER_STEP * TABLE_GROUP * TABLE_PITCH, N_KEYS), F32)],
        compiler_params=_cparams("parallel"),
        name="peer_table",
    )(i1t, i2t, gt)


def _peer_experts_kernel(n_ref, w_ref, u_ref, v_ref, h_ref, o_ref, acc_ref):
    j = pl.program_id(1)

    @pl.when(j == 0)
    def _():
        acc_ref[...] = jnp.zeros_like(acc_ref)

    act = lax.dot_general(n_ref[...], u_ref[...], NT_DIMS, preferred_element_type=F32)
    m = (jax.nn.gelu(act) * w_ref[...].astype(F32)).astype(BF16)
    acc_ref[...] += jnp.dot(m, v_ref[...], preferred_element_type=F32)

    @pl.when(j == pl.num_programs(1) - 1)
    def _():
        o_ref[...] = h_ref[...] + acc_ref[...]


def peer_experts(n2, w_flat, u, v, h1, *, tm=512, te=512):
    s, d = n2.shape
    e = u.shape[0]
    tm = min(tm, s)
    return pl.pallas_call(
        _peer_experts_kernel,
        grid=(s // tm, e // te),
        in_specs=[
            pl.BlockSpec((tm, d), lambda i, j: (i, 0)),
            pl.BlockSpec((tm, te), lambda i, j: (i, j)),
            pl.BlockSpec((te, d), lambda i, j: (j, 0)),
            pl.BlockSpec((te, d), lambda i, j: (j, 0)),
            pl.BlockSpec((tm, d), lambda i, j: (i, 0)),
        ],
        out_specs=pl.BlockSpec((tm, d), lambda i, j: (i, 0)),
        out_shape=jax.ShapeDtypeStruct((s, d), F32),
        scratch_shapes=[pltpu.VMEM((tm, d), F32)],
        compiler_params=_cparams("parallel", "arbitrary"),
        name="peer_experts",
    )(n2, w_flat, u, v, h1)


def _ple_kernel(h_ref, p_ref, gple_ref, wg_ref, wp_ref, gfin_ref, o_ref, *, final):
    h = h_ref[...]
    n3 = _rms(h, gple_ref[...]).astype(BF16)
    gate = jax.nn.sigmoid(jnp.dot(n3, wg_ref[...], preferred_element_type=F32))
    pp = jnp.dot(p_ref[...], wp_ref[...], preferred_element_type=F32)
    h = h + gate * pp
    if final:
        h = _rms(h, gfin_ref[...])
    o_ref[...] = h


def ple_block(h2, p, ple_norm, w_gate, w_proj, final_norm, *, final, tm=256):
    s, d = h2.shape
    pd = p.shape[1]
    tm = min(tm, s)
    full = lambda i: (0, 0)
    return pl.pallas_call(
        functools.partial(_ple_kernel, final=final),
        grid=(s // tm,),
        in_specs=[
            pl.BlockSpec((tm, d), lambda i: (i, 0)),
            pl.BlockSpec((tm, pd), lambda i: (i, 0)),
            pl.BlockSpec((1, d), full),
            pl.BlockSpec((d, d), full),
            pl.BlockSpec((pd, d), full),
            pl.BlockSpec((1, d), full),
        ],
        out_specs=pl.BlockSpec((tm, d), lambda i: (i, 0)),
        out_shape=jax.ShapeDtypeStruct((s, d), F32),
        compiler_params=_cparams("parallel"),
        name="ple_block",
    )(h2, p, ple_norm.reshape(1, d), w_gate, w_proj, final_norm.reshape(1, d))


def _rot_cols(w):
    half = QK_ROPE_DIM // 2
    return jnp.concatenate([-w[..., half:], w[..., :half]], axis=-1)


def _layer(h, p, cos128, sin128, prm, *, final, final_norm):
    s, d = h.shape
    (attn_norm, w_in, b_gate, q_norm, w_uq, kv_norm, w_ukv, w_attn_o, conv_w, conv_b, w_rg_a, b_rg_a,
     w_rg_x, b_rg_x, lru_lambda, w_rnn_o, w_out, ffn_norm, w_peer_q, peer_subkeys, peer_u, peer_v,
     ple_norm, w_ple_gate, w_ple_proj) = prm

    o_q, o_kv, o_kr, o_x, o_y, o_g = 0, Q_LORA_RANK, Q_LORA_RANK + KV_LORA_RANK, \
        Q_LORA_RANK + KV_LORA_RANK + QK_ROPE_DIM, Q_LORA_RANK + KV_LORA_RANK + QK_ROPE_DIM + d, \
        Q_LORA_RANK + KV_LORA_RANK + QK_ROPE_DIM + 2 * d
    w_kr = w_in[:, o_kr:o_x]
    w_kr_rot = _rot_cols(w_kr)
    w_lat = jnp.concatenate([w_in[:, o_q:o_kr], w_kr, w_kr, w_kr_rot, w_kr_rot], axis=1).astype(BF16)
    w_x = w_in[:, o_x:o_y].astype(BF16)
    w_y = w_in[:, o_y:o_g].astype(BF16)
    w_g = w_in[:, o_g:].astype(BF16)
    uq = w_uq.reshape(Q_LORA_RANK, N_HEADS, QK_HEAD_DIM)
    uq_pe = uq[:, :, QK_NOPE_DIM:]
    w_q = jnp.concatenate([uq[:, :, :QK_NOPE_DIM].reshape(Q_LORA_RANK, -1),
                           uq_pe.reshape(Q_LORA_RANK, -1),
                           _rot_cols(uq_pe).reshape(Q_LORA_RANK, -1)], axis=1).astype(BF16)
    n_lat = Q_LORA_RANK + KV_LORA_RANK + 2 * LANES
    n_qn = N_HEADS * QK_NOPE_DIM
    n_qp = N_HEADS * QK_ROPE_DIM
    scale = QK_HEAD_DIM ** -0.5 * math.log2(math.e)

    n1 = rmsnorm_bf16(h, attn_norm)

    def lat_body(acc, extra, outs):
        outs[0][...] = acc[:, :Q_LORA_RANK]
        outs[1][...] = acc[:, Q_LORA_RANK:Q_LORA_RANK + KV_LORA_RANK]
        outs[2][...] = acc[:, Q_LORA_RANK + KV_LORA_RANK:]

    tm_lat = 512
    c_q, c_kv, kr4 = _mm_call(
        "in_proj_latents", n1, w_lat, tm=tm_lat, tn=n_lat, body=lat_body,
        out_shapes=[jax.ShapeDtypeStruct((s, Q_LORA_RANK), F32), jax.ShapeDtypeStruct((s, KV_LORA_RANK), F32),
                    jax.ShapeDtypeStruct((s, 2 * LANES), F32)],
        out_specs=[pl.BlockSpec((min(tm_lat, s), Q_LORA_RANK), lambda j, i: (i, 0)),
                   pl.BlockSpec((min(tm_lat, s), KV_LORA_RANK), lambda j, i: (i, 0)),
                   pl.BlockSpec((min(tm_lat, s), 2 * LANES), lambda j, i: (i, 0))])

    tm, tn = min(512, s), 1024

    def store_body(acc, extra, outs):
        outs[0][...] = acc.astype(outs[0].dtype)

    x_r = _mm_call("in_proj_x", n1, w_x, tm=tm, tn=tn, body=store_body,
                   out_shapes=jax.ShapeDtypeStruct((s, d), F32), out_specs=_tile_spec(tm, tn))

    def gelu_body(acc, extra, outs):
        outs[0][...] = jax.nn.gelu(acc).astype(outs[0].dtype)

    gy = _mm_call("in_proj_y", n1, w_y, tm=tm, tn=tn, body=gelu_body,
                  out_shapes=jax.ShapeDtypeStruct((s, d), BF16), out_specs=_tile_spec(tm, tn))

    def gate_body(acc, extra, outs):
        outs[0][...] = jax.nn.sigmoid(acc + extra[0][...]).astype(outs[0].dtype)

    gates = _mm_call("in_proj_gates", n1, w_g, tm=tm, tn=tn, body=gate_body,
                     extras=(b_gate.reshape(1, 2 * d),), extra_specs=(_col_spec(tn),),
                     out_shapes=jax.ShapeDtypeStruct((s, 2 * d), BF16), out_specs=_tile_spec(tm, tn))

    def norm_prologue(a, extra):
        return _rms(a, extra[0][...]).astype(BF16)

    def q_body(acc, extra, outs):
        outs[0][...] = (acc[:, :n_qn] * scale).astype(BF16)
        cos_t = jnp.tile(extra[1][...], (1, n_qp // LANES))
        sin_t = jnp.tile(extra[2][...], (1, n_qp // LANES))
        pe = acc[:, n_qn:n_qn + n_qp] * cos_t + acc[:, n_qn + n_qp:] * sin_t
        outs[1][...] = (pe * scale).astype(BF16)

    tm_q = min(256, s)
    row128 = pl.BlockSpec((tm_q, LANES), lambda j, i: (i, 0))
    qn, qpe = _mm_call(
        "q_proj", c_q, w_q, tm=tm_q, tn=n_qn + 2 * n_qp, prologue=norm_prologue, body=q_body,
        extras=(q_norm.reshape(1, Q_LORA_RANK), cos128, sin128),
        extra_specs=(pl.BlockSpec((1, Q_LORA_RANK), lambda j, i: (0, 0)), row128, row128),
        out_shapes=[jax.ShapeDtypeStruct((s, n_qn), BF16), jax.ShapeDtypeStruct((s, n_qp), BF16)],
        out_specs=[pl.BlockSpec((tm_q, n_qn), lambda j, i: (i, 0)), pl.BlockSpec((tm_q, n_qp), lambda j, i: (i, 0))])

    def kv_body(acc, extra, outs):
        outs[0][...] = acc.astype(BF16)
        kr = extra[1][...]
        kpe2 = kr[:, :LANES] * extra[2][...] + kr[:, LANES:] * extra[3][...]
        lane = lax.broadcasted_iota(jnp.int32, kpe2.shape, 1)
        outs[1][...] = jnp.concatenate([jnp.where(lane < QK_ROPE_DIM, kpe2, 0.0),
                                        jnp.where(lane >= QK_ROPE_DIM, kpe2, 0.0)], axis=1).astype(BF16)

    n_kv = N_HEADS * (QK_NOPE_DIM + V_HEAD_DIM)
    kv, kpe = _mm_call(
        "kv_proj", c_kv, w_ukv.astype(BF16), tm=tm_q, tn=n_kv, prologue=norm_prologue, body=kv_body,
        extras=(kv_norm.reshape(1, KV_LORA_RANK), kr4, cos128, sin128),
        extra_specs=(pl.BlockSpec((1, KV_LORA_RANK), lambda j, i: (0, 0)),
                     pl.BlockSpec((tm_q, 2 * LANES), lambda j, i: (i, 0)), row128, row128),
        out_shapes=[jax.ShapeDtypeStruct((s, n_kv), BF16), jax.ShapeDtypeStruct((s, 2 * LANES), BF16)],
        out_specs=[pl.BlockSpec((tm_q, n_kv), lambda j, i: (i, 0)), pl.BlockSpec((tm_q, 2 * LANES), lambda j, i: (i, 0))])

    o_attn = mla_attention(qn, qpe, kv, kpe)
    y_rnn_in = rglru(x_r, gy, conv_w, conv_b, w_rg_a, b_rg_a, w_rg_x, b_rg_x, lru_lambda)

    def merge_kernel(o_ref, y_ref, wo_ref, wr_ref, ga_ref, gr_ref, m_ref):
        ya = jnp.dot(o_ref[...], wo_ref[...], preferred_element_type=F32)
        yr = jnp.dot(y_ref[...], wr_ref[...], preferred_element_type=F32)
        m_ref[...] = (ga_ref[...].astype(F32) * ya + gr_ref[...].astype(F32) * yr).astype(m_ref.dtype)

    a_spec = pl.BlockSpec((tm, d), lambda j, i: (i, 0))
    w_spec = pl.BlockSpec((d, tn), lambda j, i: (0, j))
    merged = pl.pallas_call(
        merge_kernel,
        grid=(d // tn, s // tm),
        in_specs=[a_spec, a_spec, w_spec, w_spec, _tile_spec(tm, tn), _tile_spec(tm, tn, d // tn)],
        out_specs=_tile_spec(tm, tn),
        out_shape=jax.ShapeDtypeStruct((s, d), BF16),
        compiler_params=_cparams("parallel", "parallel"),
        name="mixer_merge",
    )(o_attn, y_rnn_in, w_attn_o.astype(BF16), w_rnn_o.astype(BF16), gates, gates)

    def out_body(acc, extra, outs):
        h1 = extra[0][...] + acc
        outs[0][...] = h1
        outs[1][...] = _rms(h1, extra[1][...]).astype(BF16)

    tm_o = min(256, s)
    rowd = pl.BlockSpec((tm_o, d), lambda j, i: (i, 0))
    h1, n2 = _mm_call(
        "out_proj", merged, w_out.astype(BF16), tm=tm_o, tn=d, body=out_body,
        extras=(h, ffn_norm.reshape(1, d)), extra_specs=(rowd, pl.BlockSpec((1, d), lambda j, i: (0, 0))),
        out_shapes=[jax.ShapeDtypeStruct((s, d), F32), jax.ShapeDtypeStruct((s, d), BF16)],
        out_specs=[rowd, rowd])

    n_hc = 2 * PEER_HEADS

    def pq_body(acc, extra, outs):
        for c in range(n_hc):
            outs[0][c] = acc[:, c * PEER_HALF:(c + 1) * PEER_HALF]

    pq = _mm_call("peer_query", n2, w_peer_q.astype(BF16), tm=tm_o, tn=n_hc * PEER_HALF, body=pq_body,
                  out_shapes=jax.ShapeDtypeStruct((n_hc, s, PEER_HALF), F32),
                  out_specs=pl.BlockSpec((n_hc, tm_o, PEER_HALF), lambda j, i: (0, i, 0)))
    i1t, i2t, gt = peer_route(pq, peer_subkeys.reshape(n_hc, N_KEYS, PEER_HALF).astype(BF16))
    w_tab = peer_table(i1t, i2t, gt)
    h2 = peer_experts(n2, w_tab, peer_u.astype(BF16), peer_v.astype(BF16), h1)

    return ple_block(h2, p.astype(BF16), ple_norm, w_ple_gate.astype(BF16), w_ple_proj.astype(BF16),
                     final_norm, final=final)


def kernel(x, p, attn_norm, w_in, b_gate, q_norm, w_uq, kv_norm, w_ukv, w_attn_o, conv_w, conv_b, w_rg_a, b_rg_a,
           w_rg_x, b_rg_x, lru_lambda, w_rnn_o, w_out, ffn_norm, w_peer_q, peer_subkeys, peer_u, peer_v, ple_norm,
           w_ple_gate, w_ple_proj, final_norm):
    bsz, s, d = x.shape
    assert bsz == 1, "one sequence per call"
    depth = w_in.shape[0]
    layer_params = (attn_norm, w_in, b_gate, q_norm, w_uq, kv_norm, w_ukv, w_attn_o, conv_w, conv_b, w_rg_a,
                    b_rg_a, w_rg_x, b_rg_x, lru_lambda, w_rnn_o, w_out, ffn_norm, w_peer_q, peer_subkeys, peer_u,
                    peer_v, ple_norm, w_ple_gate, w_ple_proj)

    half = QK_ROPE_DIM // 2
    inv_freq = 1.0 / (ROPE_THETA ** (jnp.arange(half, dtype=F32) / half))
    ang = jnp.arange(s, dtype=F32)[:, None] * inv_freq[None, :]
    cos128 = jnp.tile(jnp.cos(ang), (1, LANES // half))
    sin128 = jnp.tile(jnp.sin(ang), (1, LANES // half))

    h = x.reshape(s, d)
    for l in range(depth):
        h = _layer(h, p[l, 0], cos128, sin128, tuple(w[l] for w in layer_params),
                   final=(l == depth - 1), final_norm=final_norm)
    return h.reshape(bsz, s, d)
```

```python
import functools
import math

import jax
import jax.numpy as jnp
from jax import lax
from jax.experimental import pallas as pl
from jax.experimental.pallas import tpu as pltpu

RMS_EPS = 1e-6
N_HEADS = 16
QK_NOPE_DIM = 128
QK_ROPE_DIM = 64
V_HEAD_DIM = 128
QK_HEAD_DIM = QK_NOPE_DIM + QK_ROPE_DIM
Q_LORA_RANK = 768
KV_LORA_RANK = 512
ROPE_THETA = 10000.0
RNN_BLOCKS = 16
RNN_BLOCK_DIM = 128
CONV_WIDTH = 4
LRU_C = 8.0
PEER_HEADS = 8
N_KEYS = 128
PEER_HALF = 128
PEER_TOPK = 16

LANES = 128
SUBLANES = 8
VMEM_LIMIT_BYTES = 56 * 1024 * 1024

F32 = jnp.float32
BF16 = jnp.bfloat16
NT_DIMS = (((1,), (1,)), ((), ()))


def _cparams(*sem):
    return pltpu.CompilerParams(dimension_semantics=sem, vmem_limit_bytes=VMEM_LIMIT_BYTES)


def _rms(x, g):
    ms = jnp.mean(x * x, axis=-1, keepdims=True)
    return (x * lax.rsqrt(ms + RMS_EPS)) * g


def _rmsnorm_kernel(x_ref, g_ref, o_ref):
    o_ref[...] = _rms(x_ref[...], g_ref[...]).astype(o_ref.dtype)


def rmsnorm_bf16(x, g, *, tr=512):
    m, d = x.shape
    tr = min(tr, m)
    return pl.pallas_call(
        _rmsnorm_kernel,
        grid=(m // tr,),
        in_specs=[pl.BlockSpec((tr, d), lambda i: (i, 0)), pl.BlockSpec((1, d), lambda i: (0, 0))],
        out_specs=pl.BlockSpec((tr, d), lambda i: (i, 0)),
        out_shape=jax.ShapeDtypeStruct((m, d), BF16),
        compiler_params=_cparams("parallel"),
        name="rmsnorm",
    )(x, g.reshape(1, d))


def _mm_kernel(a_ref, w_ref, *refs, n_extra, prologue, body, w_rows_are_outputs):
    extra = refs[:n_extra]
    outs = refs[n_extra:]
    a = a_ref[...]
    if prologue is not None:
        a = prologue(a, extra)
    if w_rows_are_outputs:
        acc = lax.dot_general(a, w_ref[...], NT_DIMS, preferred_element_type=F32)
    else:
        acc = jnp.dot(a, w_ref[...], preferred_element_type=F32)
    body(acc, extra, outs)


def _mm_call(name, a, w, *, tm, tn, extras=(), extra_specs=(), out_shapes, out_specs, body, prologue=None,
             w_rows_are_outputs=False):
    m, k = a.shape
    n = w.shape[0] if w_rows_are_outputs else w.shape[1]
    tm = min(tm, m)
    assert m % tm == 0 and n % tn == 0
    kern = functools.partial(_mm_kernel, n_extra=len(extras), prologue=prologue, body=body,
                             w_rows_are_outputs=w_rows_are_outputs)
    w_spec = pl.BlockSpec((tn, k), lambda j, i: (j, 0)) if w_rows_are_outputs else \
        pl.BlockSpec((k, tn), lambda j, i: (0, j))
    return pl.pallas_call(
        kern,
        grid=(n // tn, m // tm),
        in_specs=[pl.BlockSpec((tm, k), lambda j, i: (i, 0)), w_spec]
        + list(extra_specs),
        out_specs=out_specs,
        out_shape=out_shapes,
        compiler_params=_cparams("parallel", "parallel"),
        name=name,
    )(a, w, *extras)


def _tile_spec(tm, tn, col_off_blocks=0):
    return pl.BlockSpec((tm, tn), lambda j, i: (i, j + col_off_blocks))


def _col_spec(tn):
    return pl.BlockSpec((1, tn), lambda j, i: (0, j))


ATTN_ROW_CHUNK = 32


def _attn_kernel(qn_ref, qpe_ref, kv_ref, kpe_ref, o_ref, s0_sc, s1_sc, p0_sc, p1_sc, a0_sc, a1_sc,
                 m_sc, l_sc, acc_sc, *, tq):
    i = pl.program_id(1)
    rc = ATTN_ROW_CHUNK
    qpe = qpe_ref[...]
    qs = [jnp.concatenate([qn_ref[:, hh * LANES:(hh + 1) * LANES], qpe], axis=1) for hh in range(2)]
    m_sc[...] = jnp.full(m_sc.shape, -jnp.inf, F32)
    l_sc[...] = jnp.zeros(l_sc.shape, F32)
    acc_sc[...] = jnp.zeros(acc_sc.shape, F32)
    chunks = [slice(c * rc, (c + 1) * rc) for c in range(tq // rc)]
    even = (s0_sc, p0_sc, a0_sc)
    odd = (s1_sc, p1_sc, a1_sc)

    def scores(kb, bufs):
        s_sc = bufs[0]
        start = pl.multiple_of(kb * tq, tq)
        for hh in range(2):
            k = jnp.concatenate(
                [kv_ref[pl.ds(start, tq), hh * 256:hh * 256 + LANES],
                 kpe_ref[pl.ds(start, tq), hh * LANES:(hh + 1) * LANES]], axis=1)
            s_sc[hh] = lax.dot_general(qs[hh], k, NT_DIMS, preferred_element_type=F32)

    def mask_diagonal(bufs):
        s_sc = bufs[0]
        for hh in range(2):
            for rows in chunks:
                row = lax.broadcasted_iota(jnp.int32, (rc, tq), 0) + rows.start
                col = lax.broadcasted_iota(jnp.int32, (rc, tq), 1)
                s_sc[hh, rows, :] = jnp.where(col <= row, s_sc[hh, rows, :], -jnp.inf)

    def accumulate(kb, bufs):
        s_sc, p_sc, a_sc = bufs
        start = pl.multiple_of(kb * tq, tq)
        for hh in range(2):
            for rows in chunks:
                m_old = m_sc[hh, rows, :]
                m_new = jnp.maximum(m_old, jnp.max(s_sc[hh, rows, :], axis=-1, keepdims=True))
                a_sc[hh, rows, :] = jnp.exp2(m_old - m_new)
                m_sc[hh, rows, :] = m_new
            for rows in chunks:
                m_new = jnp.concatenate([m_sc[hh, rows, :]] * (tq // LANES), axis=1)
                p = jnp.exp2(s_sc[hh, rows, :] - m_new)
                l_sc[hh, rows, :] = a_sc[hh, rows, :] * l_sc[hh, rows, :] + jnp.sum(p, axis=-1, keepdims=True)
                p_sc[hh, rows, :] = p.astype(BF16)
        for hh in range(2):
            v = kv_ref[pl.ds(start, tq), hh * 256 + LANES:(hh + 1) * 256]
            acc_sc[hh] = a_sc[hh] * acc_sc[hh] + jnp.dot(p_sc[hh], v, preferred_element_type=F32)

    scores(0, even)

    def two_blocks(kk, carry):
        kb = 2 * kk + 1
        scores(kb, odd)
        accumulate(kb - 1, even)
        scores(kb + 1, even)
        accumulate(kb, odd)
        return carry

    lax.fori_loop(0, i // 2, two_blocks, 0)

    @pl.when(i % 2 == 1)
    def _():
        scores(i, odd)
        accumulate(i - 1, even)
        mask_diagonal(odd)
        accumulate(i, odd)

    @pl.when(i % 2 == 0)
    def _():
        mask_diagonal(even)
        accumulate(i, even)

    for hh in range(2):
        o_ref[:, hh * LANES:(hh + 1) * LANES] = (acc_sc[hh] / l_sc[hh]).astype(o_ref.dtype)


def mla_attention(qn, qpe, kv, kpe, *, tq=512):
    s = qn.shape[0]
    tq = min(tq, s)
    n_pairs = N_HEADS // 2
    return pl.pallas_call(
        functools.partial(_attn_kernel, tq=tq),
        grid=(n_pairs, s // tq),
        in_specs=[
            pl.BlockSpec((tq, 2 * LANES), lambda j, i: (i, j)),
            pl.BlockSpec((tq, LANES), lambda j, i: (i, j)),
            pl.BlockSpec((s, 4 * LANES), lambda j, i: (0, j)),
            pl.BlockSpec((s, 2 * LANES), lambda j, i: (0, 0)),
        ],
        out_specs=pl.BlockSpec((tq, 2 * LANES), lambda j, i: (i, j)),
        out_shape=jax.ShapeDtypeStruct((s, N_HEADS * V_HEAD_DIM), BF16),
        scratch_shapes=[pltpu.VMEM((2, tq, tq), F32), pltpu.VMEM((2, tq, tq), F32),
                        pltpu.VMEM((2, tq, tq), BF16), pltpu.VMEM((2, tq, tq), BF16),
                        pltpu.VMEM((2, tq, LANES), F32), pltpu.VMEM((2, tq, LANES), F32),
                        pltpu.VMEM((2, tq, LANES), F32), pltpu.VMEM((2, tq, LANES), F32),
                        pltpu.VMEM((2, tq, V_HEAD_DIM), F32)],
        compiler_params=_cparams("parallel", "parallel"),
        name="mla_attention",
    )(qn, qpe, kv, kpe)


def _softplus(z):
    return jnp.maximum(z, 0.0) + jnp.log1p(jnp.exp(-jnp.abs(z)))


def _rglru_kernel(x_ref, gy_ref, cw_ref, cb_ref, wa_ref, ba_ref, wx_ref, bx_ref, lam_ref, y_ref,
                  h_sc, prev_sc, *, ts):
    tb = pl.program_id(1)

    @pl.when(tb == 0)
    def _():
        h_sc[...] = jnp.zeros_like(h_sc)
        prev_sc[...] = jnp.zeros_like(prev_sc)

    x = x_ref[...]
    prev = prev_sc[...]
    cw = cw_ref[...]
    row8 = lax.broadcasted_iota(jnp.int32, (SUBLANES, LANES), 0)
    xc = x * cw[CONV_WIDTH - 1:CONV_WIDTH, :] + cb_ref[...]
    for d in range(1, CONV_WIDTH):
        rolled = pltpu.roll(x, d, 0)
        fix = pltpu.roll(prev, d, 0)
        head = jnp.where(row8 < d, fix, rolled[:SUBLANES])
        xs = jnp.concatenate([head, rolled[SUBLANES:]], axis=0)
        xc = xc + xs * cw[CONV_WIDTH - 1 - d:CONV_WIDTH - d, :]
    prev_sc[...] = x[ts - SUBLANES:, :]

    xcb = xc.astype(BF16)
    r = jax.nn.sigmoid(jnp.dot(xcb, wa_ref[...], preferred_element_type=F32) + ba_ref[...])
    ig = jax.nn.sigmoid(jnp.dot(xcb, wx_ref[...], preferred_element_type=F32) + bx_ref[...])
    log_a = (-LRU_C) * r * _softplus(-lam_ref[...])
    a = jnp.exp(log_a)
    th = jnp.tanh(log_a)
    mult = jnp.sqrt(-2.0 * th / (1.0 - th))
    rowt = lax.broadcasted_iota(jnp.int32, (ts, LANES), 0)
    mult = jnp.where(rowt + tb * ts == 0, 1.0, mult)
    b = mult * (ig * xc)

    d = 1
    while d < ts:
        if d < SUBLANES:
            a_sh = pltpu.roll(a, d, 0)
            b_sh = pltpu.roll(b, d, 0)
            keep = rowt >= d
            b = jnp.where(keep, a * b_sh + b, b)
            a = jnp.where(keep, a * a_sh, a)
        else:
            a_sh = jnp.concatenate([jnp.ones((d, LANES), F32), a[:ts - d]], axis=0)
            b_sh = jnp.concatenate([jnp.zeros((d, LANES), F32), b[:ts - d]], axis=0)
            b = a * b_sh + b
            a = a * a_sh
        d *= 2
    h = a * h_sc[0:1, :] + b
    h_sc[...] = jnp.broadcast_to(h[ts - 1:ts, :], h_sc.shape)
    y_ref[...] = (h * gy_ref[...].astype(F32)).astype(y_ref.dtype)


def rglru(x_r, gy, conv_w, conv_b, w_rg_a, b_rg_a, w_rg_x, b_rg_x, lru_lambda, *, ts=512):
    s, c = x_r.shape
    ts = min(ts, s)
    nb = c // LANES
    col = lambda cb, tb: (0, cb)
    blk = lambda cb, tb: (cb, 0, 0)
    return pl.pallas_call(
        functools.partial(_rglru_kernel, ts=ts),
        grid=(nb, s // ts),
        in_specs=[
            pl.BlockSpec((ts, LANES), lambda cb, tb: (tb, cb)),
            pl.BlockSpec((ts, LANES), lambda cb, tb: (tb, cb)),
            pl.BlockSpec((CONV_WIDTH, LANES), col),
            pl.BlockSpec((1, LANES), col),
            pl.BlockSpec((None, RNN_BLOCK_DIM, RNN_BLOCK_DIM), blk),
            pl.BlockSpec((None, 1, RNN_BLOCK_DIM), blk),
            pl.BlockSpec((None, RNN_BLOCK_DIM, RNN_BLOCK_DIM), blk),
            pl.BlockSpec((None, 1, RNN_BLOCK_DIM), blk),
            pl.BlockSpec((1, LANES), col),
        ],
        out_specs=pl.BlockSpec((ts, LANES), lambda cb, tb: (tb, cb)),
        out_shape=jax.ShapeDtypeStruct((s, c), BF16),
        scratch_shapes=[pltpu.VMEM((SUBLANES, LANES), F32), pltpu.VMEM((SUBLANES, LANES), F32)],
        compiler_params=_cparams("parallel", "arbitrary"),
        name="rglru",
    )(x_r, gy, conv_w, conv_b.reshape(1, c), w_rg_a.astype(BF16), b_rg_a.reshape(nb, 1, RNN_BLOCK_DIM),
      w_rg_x.astype(BF16), b_rg_x.reshape(nb, 1, RNN_BLOCK_DIM), lru_lambda.reshape(1, c))


def _sorting_network(n):
    pairs = []

    def merge(lo, length, r):
        step = r * 2
        if step < length:
            merge(lo, length, step)
            merge(lo + r, length, step)
            pairs.extend((i, i + r) for i in range(lo + r, lo + length - r, step))
        else:
            pairs.append((lo, lo + r))

    def sort(lo, length):
        if length > 1:
            half = length // 2
            sort(lo, half)
            sort(lo + half, half)
            merge(lo, length, 1)

    sort(0, n)
    return pairs


def _topk_keys(s, k, vals_ref, idx_ref):
    n_slabs = s.shape[0] // SUBLANES
    assert n_slabs == k
    sub = lax.broadcasted_iota(jnp.int32, (SUBLANES, s.shape[1]), 0)
    vals = [s[v * SUBLANES:(v + 1) * SUBLANES, :] for v in range(n_slabs)]
    idxs = [sub + v * SUBLANES for v in range(n_slabs)]
    for i, j in _sorting_network(n_slabs):
        first = (vals[i] > vals[j]) | ((vals[i] == vals[j]) & (idxs[i] < idxs[j]))
        vals[i], vals[j] = jnp.where(first, vals[i], vals[j]), jnp.where(first, vals[j], vals[i])
        idxs[i], idxs[j] = jnp.where(first, idxs[i], idxs[j]), jnp.where(first, idxs[j], idxs[i])
    for r in range(k):
        m = jnp.max(vals[0], axis=0, keepdims=True)
        im = jnp.min(jnp.where(vals[0] == m, idxs[0], N_KEYS), axis=0, keepdims=True)
        vals_ref[r:r + 1, :] = m
        idx_ref[r:r + 1, :] = im
        taken = idxs[0] == im
        for v in range(k - 1 - r):
            vals[v] = jnp.where(taken, vals[v + 1], vals[v])
            idxs[v] = jnp.where(taken, idxs[v + 1], idxs[v])


def _pair_candidates(v0_sc, v1_sc):
    k = PEER_TOPK
    tt = v0_sc.shape[1]
    row8 = lax.broadcasted_iota(jnp.int32, (SUBLANES, tt), 0)
    v0_lo = v0_sc[0:SUBLANES, :]
    vals = [v0_lo + v1_sc[0:1, :], v0_sc[SUBLANES:k, :] + v1_sc[0:1, :]]
    flat = [row8 * k, (row8 + SUBLANES) * k]
    for b in range(1, SUBLANES):
        ok = row8 < k // (b + 1)
        vals.append(jnp.where(ok, v0_lo + v1_sc[b:b + 1, :], -jnp.inf))
        flat.append(jnp.where(ok, row8 * k + b, k * k))
    vals.append(v0_sc[0:1, :] + v1_sc[SUBLANES:k, :])
    flat.append(row8 + SUBLANES)
    return jnp.concatenate(vals, axis=0), jnp.concatenate(flat, axis=0)


def _peer_route_kernel(pq_ref, sk_ref, i1_ref, i2_ref, g_ref, v0_sc, i0_sc, v1_sc, j1_sc, bs_sc, bc_sc):
    k = PEER_TOPK
    for c, (v_sc, i_sc) in enumerate(((v0_sc, i0_sc), (v1_sc, j1_sc))):
        s_t = lax.dot_general(sk_ref[c], pq_ref[c].astype(BF16), NT_DIMS, preferred_element_type=F32)
        _topk_keys(s_t, k, v_sc, i_sc)
    cand, flat = _pair_candidates(v0_sc, v1_sc)
    for r in range(k):
        m = jnp.max(cand, axis=0, keepdims=True)
        am = jnp.min(jnp.where(cand == m, flat, k * k), axis=0, keepdims=True)
        bs_sc[r:r + 1, :] = m
        bc_sc[r:r + 1, :] = am
        cand = jnp.where(flat == am, -jnp.inf, cand)
    bs = bs_sc[...]
    bc = bc_sc[...]
    row = lax.broadcasted_iota(jnp.int32, bs.shape, 0)
    i0 = i0_sc[...]
    j1 = j1_sc[...]
    for r in range(k):
        ca = bc[r:r + 1, :] >> 4
        cb = bc[r:r + 1, :] & (k - 1)
        i1_ref[r:r + 1, :] = jnp.sum(jnp.where(row == ca, i0, 0), axis=0, keepdims=True)
        i2_ref[r:r + 1, :] = jnp.sum(jnp.where(row == cb, j1, 0), axis=0, keepdims=True)
    e = jnp.exp(bs - bs[0:1, :])
    g_ref[...] = e / jnp.sum(e, axis=0, keepdims=True)


def peer_route(pq, subkeys, *, tt=256):
    s = pq.shape[1]
    tt = min(tt, s)
    k = PEER_TOPK
    out = jax.ShapeDtypeStruct((PEER_HEADS * k, s), jnp.int32)
    ospec = pl.BlockSpec((k, tt), lambda i, h: (h, i))
    return pl.pallas_call(
        _peer_route_kernel,
        grid=(s // tt, PEER_HEADS),
        in_specs=[pl.BlockSpec((2, tt, PEER_HALF), lambda i, h: (h, i, 0)),
                  pl.BlockSpec((2, N_KEYS, PEER_HALF), lambda i, h: (h, 0, 0))],
        out_specs=[ospec, ospec, ospec],
        out_shape=[out, out, jax.ShapeDtypeStruct((PEER_HEADS * k, s), F32)],
        scratch_shapes=[pltpu.VMEM((k, tt), F32), pltpu.VMEM((k, tt), jnp.int32),
                        pltpu.VMEM((k, tt), F32), pltpu.VMEM((k, tt), jnp.int32),
                        pltpu.VMEM((k, tt), F32), pltpu.VMEM((k, tt), jnp.int32)],
        compiler_params=_cparams("parallel", "parallel"),
        name="peer_route",
    )(pq, subkeys)


TABLE_GROUP = 16
TABLE_PITCH = N_KEYS + SUBLANES
TABLE_GROUPS_PER_STEP = 2


def _peer_table_kernel(i1t_ref, i2t_ref, gt_ref, w_ref, i1_sc, i2_sc, g_sc, stage_sc, *, tw):
    i1_sc[...] = i1t_ref[...].T
    i2_sc[...] = i2t_ref[...].T
    g_sc[...] = gt_ref[...].T
    sub = lax.broadcasted_iota(jnp.int32, (N_KEYS, PEER_HEADS * PEER_TOPK), 0)

    def groups(gi, carry):
        for sg in range(TABLE_GROUPS_PER_STEP):
            t0 = pl.multiple_of((gi * TABLE_GROUPS_PER_STEP + sg) * TABLE_GROUP, TABLE_GROUP)
            base = sg * TABLE_GROUP * TABLE_PITCH
            for u in range(TABLE_GROUP):
                r1 = i1_sc[pl.ds(t0 + u, 1), :]
                r2 = i2_sc[pl.ds(t0 + u, 1), :]
                gg = g_sc[pl.ds(t0 + u, 1), :]
                a = jnp.where(sub == r1, gg, 0.0).astype(BF16)
                b = jnp.where(sub == r2, 1.0, 0.0).astype(BF16)
                row0 = base + u * TABLE_PITCH
                stage_sc[row0:row0 + N_KEYS, :] = lax.dot_general(
                    a, b, NT_DIMS, preferred_element_type=F32)
            for k1 in range(N_KEYS):
                blk = stage_sc[pl.ds(base + k1, TABLE_GROUP, stride=TABLE_PITCH), :]
                w_ref[pl.ds(t0, TABLE_GROUP), k1 * N_KEYS:(k1 + 1) * N_KEYS] = blk.astype(w_ref.dtype)
        return carry

    lax.fori_loop(0, tw // (TABLE_GROUP * TABLE_GROUPS_PER_STEP), groups, 0)


def peer_table(i1t, i2t, gt, *, tw=256):
    hk, s = i1t.shape
    tw = min(tw, s)
    assert tw % (TABLE_GROUP * TABLE_GROUPS_PER_STEP) == 0
    ispec = pl.BlockSpec((hk, tw), lambda i: (0, i))
    return pl.pallas_call(
        functools.partial(_peer_table_kernel, tw=tw),
        grid=(s // tw,),
        in_specs=[ispec, ispec, ispec],
        out_specs=pl.BlockSpec((tw, N_KEYS * N_KEYS), lambda i: (i, 0)),
        out_shape=jax.ShapeDtypeStruct((s, N_KEYS * N_KEYS), BF16),
        scratch_shapes=[pltpu.VMEM((tw, hk), jnp.int32), pltpu.VMEM((tw, hk), jnp.int32),
                        pltpu.VMEM((tw, hk), F32),
                        pltpu.VMEM((TABLE_GROUPS_PER_STEP * TABLE_GROUP * TABLE_PITCH, N_KEYS), F32)],
        compiler_params=_cparams("parallel"),
        name="peer_table",
    )(i1t, i2t, gt)


def _peer_experts_kernel(n_ref, w_ref, u_ref, v_ref, h_ref, o_ref, acc_ref):
    j = pl.program_id(1)

    @pl.when(j == 0)
    def _():
        acc_ref[...] = jnp.zeros_like(acc_ref)

    act = lax.dot_general(n_ref[...], u_ref[...], NT_DIMS, preferred_element_type=F32)
    m = (jax.nn.gelu(act) * w_ref[...].astype(F32)).astype(BF16)
    acc_ref[...] += jnp.dot(m, v_ref[...], preferred_element_type=F32)

    @pl.when(j == pl.num_programs(1) - 1)
    def _():
        o_ref[...] = h_ref[...] + acc_ref[...]


def peer_experts(n2, w_flat, u, v, h1, *, tm=512, te=1024):
    s, d = n2.shape
    e = u.shape[0]
    tm = min(tm, s)
    return pl.pallas_call(
        _peer_experts_kernel,
        grid=(s // tm, e // te),
        in_specs=[
            pl.BlockSpec((tm, d), lambda i, j: (i, 0)),
            pl.BlockSpec((tm, te), lambda i, j: (i, j)),
            pl.BlockSpec((te, d), lambda i, j: (j, 0)),
            pl.BlockSpec((te, d), lambda i, j: (j, 0)),
            pl.BlockSpec((tm, d), lambda i, j: (i, 0)),
        ],
        out_specs=pl.BlockSpec((tm, d), lambda i, j: (i, 0)),
        out_shape=jax.ShapeDtypeStruct((s, d), F32),
        scratch_shapes=[pltpu.VMEM((tm, d), F32)],
        compiler_params=_cparams("parallel", "arbitrary"),
        name="peer_experts",
    )(n2, w_flat, u, v, h1)


def _ple_kernel(h_ref, p_ref, gple_ref, wg_ref, wp_ref, gfin_ref, o_ref, *, final):
    h = h_ref[...]
    n3 = _rms(h, gple_ref[...]).astype(BF16)
    gate = jax.nn.sigmoid(jnp.dot(n3, wg_ref[...], preferred_element_type=F32))
    pp = jnp.dot(p_ref[...], wp_ref[...], preferred_element_type=F32)
    h = h + gate * pp
    if final:
        h = _rms(h, gfin_ref[...])
    o_ref[...] = h


def ple_block(h2, p, ple_norm, w_gate, w_proj, final_norm, *, final, tm=256):
    s, d = h2.shape
    pd = p.shape[1]
    tm = min(tm, s)
    full = lambda i: (0, 0)
    return pl.pallas_call(
        functools.partial(_ple_kernel, final=final),
        grid=(s // tm,),
        in_specs=[
            pl.BlockSpec((tm, d), lambda i: (i, 0)),
            pl.BlockSpec((tm, pd), lambda i: (i, 0)),
            pl.BlockSpec((1, d), full),
            pl.BlockSpec((d, d), full),
            pl.BlockSpec((pd, d), full),
            pl.BlockSpec((1, d), full),
        ],
        out_specs=pl.BlockSpec((tm, d), lambda i: (i, 0)),
        out_shape=jax.ShapeDtypeStruct((s, d), F32),
        compiler_params=_cparams("parallel"),
        name="ple_block",
    )(h2, p, ple_norm.reshape(1, d), w_gate, w_proj, final_norm.reshape(1, d))


def _rot_cols(w):
    half = QK_ROPE_DIM // 2
    return jnp.concatenate([-w[..., half:], w[..., :half]], axis=-1)


def _layer(h, p, cos128, sin128, prm, *, final, final_norm):
    s, d = h.shape
    (attn_norm, w_in, b_gate, q_norm, w_uq, kv_norm, w_ukv, w_attn_o, conv_w, conv_b, w_rg_a, b_rg_a,
     w_rg_x, b_rg_x, lru_lambda, w_rnn_o, w_out, ffn_norm, w_peer_q, peer_subkeys, peer_u, peer_v,
     ple_norm, w_ple_gate, w_ple_proj) = prm

    o_q, o_kv, o_kr, o_x, o_y, o_g = 0, Q_LORA_RANK, Q_LORA_RANK + KV_LORA_RANK, \
        Q_LORA_RANK + KV_LORA_RANK + QK_ROPE_DIM, Q_LORA_RANK + KV_LORA_RANK + QK_ROPE_DIM + d, \
        Q_LORA_RANK + KV_LORA_RANK + QK_ROPE_DIM + 2 * d
    w_in_t = w_in.T
    w_kr = w_in_t[o_kr:o_x]
    w_kr_rot = _rot_cols(w_kr.T).T
    w_lat = jnp.concatenate([w_in_t[o_q:o_kr], w_kr, w_kr, w_kr_rot, w_kr_rot], axis=0).astype(BF16)
    w_x = w_in_t[o_x:o_y].astype(BF16)
    w_y = w_in_t[o_y:o_g].astype(BF16)
    w_g = w_in_t[o_g:].astype(BF16)
    uq = w_uq.reshape(Q_LORA_RANK, N_HEADS, QK_HEAD_DIM)
    uq_pe = uq[:, :, QK_NOPE_DIM:]
    w_q = jnp.concatenate([uq[:, :, :QK_NOPE_DIM].reshape(Q_LORA_RANK, -1),
                           uq_pe.reshape(Q_LORA_RANK, -1),
                           _rot_cols(uq_pe).reshape(Q_LORA_RANK, -1)], axis=1).astype(BF16)
    n_lat = Q_LORA_RANK + KV_LORA_RANK + 2 * LANES
    n_qn = N_HEADS * QK_NOPE_DIM
    n_qp = N_HEADS * QK_ROPE_DIM
    scale = QK_HEAD_DIM ** -0.5 * math.log2(math.e)

    n1 = rmsnorm_bf16(h, attn_norm)

    def lat_body(acc, extra, outs):
        outs[0][...] = acc[:, :Q_LORA_RANK]
        outs[1][...] = acc[:, Q_LORA_RANK:Q_LORA_RANK + KV_LORA_RANK]
        outs[2][...] = acc[:, Q_LORA_RANK + KV_LORA_RANK:]

    tm_lat = 512
    c_q, c_kv, kr4 = _mm_call(
        "in_proj_latents", n1, w_lat, tm=tm_lat, tn=n_lat, body=lat_body, w_rows_are_outputs=True,
        out_shapes=[jax.ShapeDtypeStruct((s, Q_LORA_RANK), F32), jax.ShapeDtypeStruct((s, KV_LORA_RANK), F32),
                    jax.ShapeDtypeStruct((s, 2 * LANES), F32)],
        out_specs=[pl.BlockSpec((min(tm_lat, s), Q_LORA_RANK), lambda j, i: (i, 0)),
                   pl.BlockSpec((min(tm_lat, s), KV_LORA_RANK), lambda j, i: (i, 0)),
                   pl.BlockSpec((min(tm_lat, s), 2 * LANES), lambda j, i: (i, 0))])

    tm, tn = min(512, s), 1024

    def store_body(acc, extra, outs):
        outs[0][...] = acc.astype(outs[0].dtype)

    x_r = _mm_call("in_proj_x", n1, w_x, tm=tm, tn=tn, body=store_body, w_rows_are_outputs=True,
                   out_shapes=jax.ShapeDtypeStruct((s, d), F32), out_specs=_tile_spec(tm, tn))

    def gelu_body(acc, extra, outs):
        outs[0][...] = jax.nn.gelu(acc).astype(outs[0].dtype)

    gy = _mm_call("in_proj_y", n1, w_y, tm=tm, tn=tn, body=gelu_body, w_rows_are_outputs=True,
                  out_shapes=jax.ShapeDtypeStruct((s, d), BF16), out_specs=_tile_spec(tm, tn))

    def gate_body(acc, extra, outs):
        outs[0][...] = jax.nn.sigmoid(acc + extra[0][...]).astype(outs[0].dtype)

    gates = _mm_call("in_proj_gates", n1, w_g, tm=tm, tn=tn, body=gate_body, w_rows_are_outputs=True,
                     extras=(b_gate.reshape(1, 2 * d),), extra_specs=(_col_spec(tn),),
                     out_shapes=jax.ShapeDtypeStruct((s, 2 * d), BF16), out_specs=_tile_spec(tm, tn))

    def norm_prologue(a, extra):
        return _rms(a, extra[0][...]).astype(BF16)

    def q_body(acc, extra, outs):
        outs[0][...] = (acc[:, :n_qn] * scale).astype(BF16)
        cos_t = jnp.tile(extra[1][...], (1, n_qp // LANES))
        sin_t = jnp.tile(extra[2][...], (1, n_qp // LANES))
        pe = acc[:, n_qn:n_qn + n_qp] * cos_t + acc[:, n_qn + n_qp:] * sin_t
        outs[1][...] = (pe * scale).astype(BF16)

    tm_q = min(256, s)
    row128 = pl.BlockSpec((tm_q, LANES), lambda j, i: (i, 0))
    qn, qpe = _mm_call(
        "q_proj", c_q, w_q, tm=tm_q, tn=n_qn + 2 * n_qp, prologue=norm_prologue, body=q_body,
        extras=(q_norm.reshape(1, Q_LORA_RANK), cos128, sin128),
        extra_specs=(pl.BlockSpec((1, Q_LORA_RANK), lambda j, i: (0, 0)), row128, row128),
        out_shapes=[jax.ShapeDtypeStruct((s, n_qn), BF16), jax.ShapeDtypeStruct((s, n_qp), BF16)],
        out_specs=[pl.BlockSpec((tm_q, n_qn), lambda j, i: (i, 0)), pl.BlockSpec((tm_q, n_qp), lambda j, i: (i, 0))])

    def kv_body(acc, extra, outs):
        outs[0][...] = acc.astype(BF16)
        kr = extra[1][...]
        kpe2 = kr[:, :LANES] * extra[2][...] + kr[:, LANES:] * extra[3][...]
        lane = lax.broadcasted_iota(jnp.int32, kpe2.shape, 1)
        outs[1][...] = jnp.concatenate([jnp.where(lane < QK_ROPE_DIM, kpe2, 0.0),
                                        jnp.where(lane >= QK_ROPE_DIM, kpe2, 0.0)], axis=1).astype(BF16)

    n_kv = N_HEADS * (QK_NOPE_DIM + V_HEAD_DIM)
    kv, kpe = _mm_call(
        "kv_proj", c_kv, w_ukv.astype(BF16), tm=tm_q, tn=n_kv, prologue=norm_prologue, body=kv_body,
        extras=(kv_norm.reshape(1, KV_LORA_RANK), kr4, cos128, sin128),
        extra_specs=(pl.BlockSpec((1, KV_LORA_RANK), lambda j, i: (0, 0)),
                     pl.BlockSpec((tm_q, 2 * LANES), lambda j, i: (i, 0)), row128, row128),
        out_shapes=[jax.ShapeDtypeStruct((s, n_kv), BF16), jax.ShapeDtypeStruct((s, 2 * LANES), BF16)],
        out_specs=[pl.BlockSpec((tm_q, n_kv), lambda j, i: (i, 0)), pl.BlockSpec((tm_q, 2 * LANES), lambda j, i: (i, 0))])

    o_attn = mla_attention(qn, qpe, kv, kpe)
    y_rnn_in = rglru(x_r, gy, conv_w, conv_b, w_rg_a, b_rg_a, w_rg_x, b_rg_x, lru_lambda)

    def merge_kernel(o_ref, y_ref, wo_ref, wr_ref, ga_ref, gr_ref, m_ref):
        ya = jnp.dot(o_ref[...], wo_ref[...], preferred_element_type=F32)
        yr = jnp.dot(y_ref[...], wr_ref[...], preferred_element_type=F32)
        m_ref[...] = (ga_ref[...].astype(F32) * ya + gr_ref[...].astype(F32) * yr).astype(m_ref.dtype)

    a_spec = pl.BlockSpec((tm, d), lambda j, i: (i, 0))
    w_spec = pl.BlockSpec((d, tn), lambda j, i: (0, j))
    merged = pl.pallas_call(
        merge_kernel,
        grid=(d // tn, s // tm),
        in_specs=[a_spec, a_spec, w_spec, w_spec, _tile_spec(tm, tn), _tile_spec(tm, tn, d // tn)],
        out_specs=_tile_spec(tm, tn),
        out_shape=jax.ShapeDtypeStruct((s, d), BF16),
        compiler_params=_cparams("parallel", "parallel"),
        name="mixer_merge",
    )(o_attn, y_rnn_in, w_attn_o.astype(BF16), w_rnn_o.astype(BF16), gates, gates)

    def out_body(acc, extra, outs):
        h1 = extra[0][...] + acc
        outs[0][...] = h1
        outs[1][...] = _rms(h1, extra[1][...]).astype(BF16)

    tm_o = min(256, s)
    rowd = pl.BlockSpec((tm_o, d), lambda j, i: (i, 0))
    h1, n2 = _mm_call(
        "out_proj", merged, w_out.astype(BF16), tm=tm_o, tn=d, body=out_body,
        extras=(h, ffn_norm.reshape(1, d)), extra_specs=(rowd, pl.BlockSpec((1, d), lambda j, i: (0, 0))),
        out_shapes=[jax.ShapeDtypeStruct((s, d), F32), jax.ShapeDtypeStruct((s, d), BF16)],
        out_specs=[rowd, rowd])

    n_hc = 2 * PEER_HEADS

    def pq_body(acc, extra, outs):
        for c in range(n_hc):
            outs[0][c] = acc[:, c * PEER_HALF:(c + 1) * PEER_HALF]

    pq = _mm_call("peer_query", n2, w_peer_q.astype(BF16), tm=tm_o, tn=n_hc * PEER_HALF, body=pq_body,
                  out_shapes=jax.ShapeDtypeStruct((n_hc, s, PEER_HALF), F32),
                  out_specs=pl.BlockSpec((n_hc, tm_o, PEER_HALF), lambda j, i: (0, i, 0)))
    i1t, i2t, gt = peer_route(pq, peer_subkeys.reshape(n_hc, N_KEYS, PEER_HALF).astype(BF16))
    w_tab = peer_table(i1t, i2t, gt)
    h2 = peer_experts(n2, w_tab, peer_u.astype(BF16), peer_v.astype(BF16), h1)

    return ple_block(h2, p.astype(BF16), ple_norm, w_ple_gate.astype(BF16), w_ple_proj.astype(BF16),
                     final_norm, final=final)


def kernel(x, p, attn_norm, w_in, b_gate, q_norm, w_uq, kv_norm, w_ukv, w_attn_o, conv_w, conv_b, w_rg_a, b_rg_a,
           w_rg_x, b_rg_x, lru_lambda, w_rnn_o, w_out, ffn_norm, w_peer_q, peer_subkeys, peer_u, peer_v, ple_norm,
           w_ple_gate, w_ple_proj, final_norm):
    bsz, s, d = x.shape
    assert bsz == 1, "one sequence per call"
    depth = w_in.shape[0]
    layer_params = (attn_norm, w_in, b_gate, q_norm, w_uq, kv_norm, w_ukv, w_attn_o, conv_w, conv_b, w_rg_a,
                    b_rg_a, w_rg_x, b_rg_x, lru_lambda, w_rnn_o, w_out, ffn_norm, w_peer_q, peer_subkeys, peer_u,
                    peer_v, ple_norm, w_ple_gate, w_ple_proj)

    half = QK_ROPE_DIM // 2
    inv_freq = 1.0 / (ROPE_THETA ** (jnp.arange(half, dtype=F32) / half))
    ang = jnp.arange(s, dtype=F32)[:, None] * inv_freq[None, :]
    cos128 = jnp.tile(jnp.cos(ang), (1, LANES // half))
    sin128 = jnp.tile(jnp.sin(ang), (1, LANES // half))

    h = x.reshape(s, d)
    for l in range(depth):
        h = _layer(h, p[l, 0], cos128, sin128, tuple(w[l] for w in layer_params),
                   final=(l == depth - 1), final_norm=final_norm)
    return h.reshape(bsz, s, d)
```

```python
import functools
import math

import jax
import jax.numpy as jnp
from jax import lax
from jax.experimental import pallas as pl
from jax.experimental.pallas import tpu as pltpu

RMS_EPS = 1e-6
N_HEADS = 16
QK_NOPE_DIM = 128
QK_ROPE_DIM = 64
V_HEAD_DIM = 128
QK_HEAD_DIM = QK_NOPE_DIM + QK_ROPE_DIM
Q_LORA_RANK = 768
KV_LORA_RANK = 512
ROPE_THETA = 10000.0
RNN_BLOCKS = 16
RNN_BLOCK_DIM = 128
CONV_WIDTH = 4
LRU_C = 8.0
PEER_HEADS = 8
N_KEYS = 128
PEER_HALF = 128
PEER_TOPK = 16

LANES = 128
SUBLANES = 8
VMEM_LIMIT_BYTES = 56 * 1024 * 1024

F32 = jnp.float32
BF16 = jnp.bfloat16
NT_DIMS = (((1,), (1,)), ((), ()))


def _cparams(*sem):
    return pltpu.CompilerParams(dimension_semantics=sem, vmem_limit_bytes=VMEM_LIMIT_BYTES)


def _rms(x, g):
    ms = jnp.mean(x * x, axis=-1, keepdims=True)
    return (x * lax.rsqrt(ms + RMS_EPS)) * g


def _rmsnorm_kernel(x_ref, g_ref, o_ref):
    o_ref[...] = _rms(x_ref[...], g_ref[...]).astype(o_ref.dtype)


def rmsnorm_bf16(x, g, *, tr=512):
    m, d = x.shape
    tr = min(tr, m)
    return pl.pallas_call(
        _rmsnorm_kernel,
        grid=(m // tr,),
        in_specs=[pl.BlockSpec((tr, d), lambda i: (i, 0)), pl.BlockSpec((1, d), lambda i: (0, 0))],
        out_specs=pl.BlockSpec((tr, d), lambda i: (i, 0)),
        out_shape=jax.ShapeDtypeStruct((m, d), BF16),
        compiler_params=_cparams("parallel"),
        name="rmsnorm",
    )(x, g.reshape(1, d))


def _mm_kernel(a_ref, w_ref, *refs, n_extra, prologue, body, w_rows_are_outputs):
    extra = refs[:n_extra]
    outs = refs[n_extra:]
    a = a_ref[...]
    if prologue is not None:
        a = prologue(a, extra)
    if w_rows_are_outputs:
        acc = lax.dot_general(a, w_ref[...], NT_DIMS, preferred_element_type=F32)
    else:
        acc = jnp.dot(a, w_ref[...], preferred_element_type=F32)
    body(acc, extra, outs)


def _mm_call(name, a, w, *, tm, tn, extras=(), extra_specs=(), out_shapes, out_specs, body, prologue=None,
             w_rows_are_outputs=False):
    m, k = a.shape
    n = w.shape[0] if w_rows_are_outputs else w.shape[1]
    tm = min(tm, m)
    assert m % tm == 0 and n % tn == 0
    kern = functools.partial(_mm_kernel, n_extra=len(extras), prologue=prologue, body=body,
                             w_rows_are_outputs=w_rows_are_outputs)
    w_spec = pl.BlockSpec((tn, k), lambda j, i: (j, 0)) if w_rows_are_outputs else \
        pl.BlockSpec((k, tn), lambda j, i: (0, j))
    return pl.pallas_call(
        kern,
        grid=(n // tn, m // tm),
        in_specs=[pl.BlockSpec((tm, k), lambda j, i: (i, 0)), w_spec]
        + list(extra_specs),
        out_specs=out_specs,
        out_shape=out_shapes,
        compiler_params=_cparams("parallel", "parallel"),
        name=name,
    )(a, w, *extras)


def _tile_spec(tm, tn, col_off_blocks=0):
    return pl.BlockSpec((tm, tn), lambda j, i: (i, j + col_off_blocks))


def _col_spec(tn):
    return pl.BlockSpec((1, tn), lambda j, i: (0, j))


ATTN_ROW_CHUNK = 32


def _attn_kernel(qn_ref, qpe_ref, kv_ref, kpe_ref, o_ref, s0_sc, s1_sc, p0_sc, p1_sc, a0_sc, a1_sc,
                 m_sc, l_sc, acc_sc, *, tq):
    i = pl.program_id(1)
    rc = ATTN_ROW_CHUNK
    qpe = qpe_ref[...]
    qs = [jnp.concatenate([qn_ref[:, hh * LANES:(hh + 1) * LANES], qpe], axis=1) for hh in range(2)]
    m_sc[...] = jnp.full(m_sc.shape, -jnp.inf, F32)
    l_sc[...] = jnp.zeros(l_sc.shape, F32)
    acc_sc[...] = jnp.zeros(acc_sc.shape, F32)
    chunks = [slice(c * rc, (c + 1) * rc) for c in range(tq // rc)]
    even = (s0_sc, p0_sc, a0_sc)
    odd = (s1_sc, p1_sc, a1_sc)

    def scores(kb, bufs):
        s_sc = bufs[0]
        start = pl.multiple_of(kb * tq, tq)
        for hh in range(2):
            k = jnp.concatenate(
                [kv_ref[pl.ds(start, tq), hh * 256:hh * 256 + LANES],
                 kpe_ref[pl.ds(start, tq), hh * LANES:(hh + 1) * LANES]], axis=1)
            s_sc[hh] = lax.dot_general(qs[hh], k, NT_DIMS, preferred_element_type=F32)

    def mask_diagonal(bufs):
        s_sc = bufs[0]
        for hh in range(2):
            for rows in chunks:
                row = lax.broadcasted_iota(jnp.int32, (rc, tq), 0) + rows.start
                col = lax.broadcasted_iota(jnp.int32, (rc, tq), 1)
                s_sc[hh, rows, :] = jnp.where(col <= row, s_sc[hh, rows, :], -jnp.inf)

    def accumulate(kb, bufs):
        s_sc, p_sc, a_sc = bufs
        start = pl.multiple_of(kb * tq, tq)
        for hh in range(2):
            for rows in chunks:
                m_old = m_sc[hh, rows, :]
                m_new = jnp.maximum(m_old, jnp.max(s_sc[hh, rows, :], axis=-1, keepdims=True))
                a_sc[hh, rows, :] = jnp.exp2(m_old - m_new)
                m_sc[hh, rows, :] = m_new
            for rows in chunks:
                m_new = jnp.concatenate([m_sc[hh, rows, :]] * (tq // LANES), axis=1)
                p = jnp.exp2(s_sc[hh, rows, :] - m_new)
                l_sc[hh, rows, :] = a_sc[hh, rows, :] * l_sc[hh, rows, :] + jnp.sum(p, axis=-1, keepdims=True)
                p_sc[hh, rows, :] = p.astype(BF16)
        for hh in range(2):
            v = kv_ref[pl.ds(start, tq), hh * 256 + LANES:(hh + 1) * 256]
            acc_sc[hh] = a_sc[hh] * acc_sc[hh] + jnp.dot(p_sc[hh], v, preferred_element_type=F32)

    scores(0, even)

    def two_blocks(kk, carry):
        kb = 2 * kk + 1
        scores(kb, odd)
        accumulate(kb - 1, even)
        scores(kb + 1, even)
        accumulate(kb, odd)
        return carry

    lax.fori_loop(0, i // 2, two_blocks, 0)

    @pl.when(i % 2 == 1)
    def _():
        scores(i, odd)
        accumulate(i - 1, even)
        mask_diagonal(odd)
        accumulate(i, odd)

    @pl.when(i % 2 == 0)
    def _():
        mask_diagonal(even)
        accumulate(i, even)

    for hh in range(2):
        o_ref[:, hh * LANES:(hh + 1) * LANES] = (acc_sc[hh] / l_sc[hh]).astype(o_ref.dtype)


def mla_attention(qn, qpe, kv, kpe, *, tq=512):
    s = qn.shape[0]
    tq = min(tq, s)
    n_pairs = N_HEADS // 2
    return pl.pallas_call(
        functools.partial(_attn_kernel, tq=tq),
        grid=(n_pairs, s // tq),
        in_specs=[
            pl.BlockSpec((tq, 2 * LANES), lambda j, i: (i, j)),
            pl.BlockSpec((tq, LANES), lambda j, i: (i, j)),
            pl.BlockSpec((s, 4 * LANES), lambda j, i: (0, j)),
            pl.BlockSpec((s, 2 * LANES), lambda j, i: (0, 0)),
        ],
        out_specs=pl.BlockSpec((tq, 2 * LANES), lambda j, i: (i, j)),
        out_shape=jax.ShapeDtypeStruct((s, N_HEADS * V_HEAD_DIM), BF16),
        scratch_shapes=[pltpu.VMEM((2, tq, tq), F32), pltpu.VMEM((2, tq, tq), F32),
                        pltpu.VMEM((2, tq, tq), BF16), pltpu.VMEM((2, tq, tq), BF16),
                        pltpu.VMEM((2, tq, LANES), F32), pltpu.VMEM((2, tq, LANES), F32),
                        pltpu.VMEM((2, tq, LANES), F32), pltpu.VMEM((2, tq, LANES), F32),
                        pltpu.VMEM((2, tq, V_HEAD_DIM), F32)],
        compiler_params=_cparams("parallel", "parallel"),
        name="mla_attention",
    )(qn, qpe, kv, kpe)


def _softplus(z):
    return jnp.maximum(z, 0.0) + jnp.log1p(jnp.exp(-jnp.abs(z)))


def _rglru_kernel(x_ref, gy_ref, cw_ref, cb_ref, wa_ref, ba_ref, wx_ref, bx_ref, lam_ref, y_ref,
                  h_sc, edge_sc, a_sc, b_sc, *, ts):
    tb = pl.program_id(1)

    @pl.when(tb == 0)
    def _():
        h_sc[...] = jnp.zeros_like(h_sc)
        edge_sc[0:SUBLANES, :] = jnp.zeros((SUBLANES, LANES), F32)
        a_sc[0:SUBLANES, :] = jnp.ones((SUBLANES, LANES), F32)
        b_sc[0:SUBLANES, :] = jnp.zeros((SUBLANES, LANES), F32)

    x = x_ref[...]
    cw = cw_ref[...]
    edge_sc[SUBLANES:2 * SUBLANES, :] = x[0:SUBLANES, :]
    xc = x * cw[CONV_WIDTH - 1:CONV_WIDTH, :] + cb_ref[...]
    for d in range(1, CONV_WIDTH):
        xs = jnp.concatenate([edge_sc[SUBLANES - d:2 * SUBLANES - d, :], x_ref[SUBLANES - d:ts - d, :]], axis=0)
        xc = xc + xs * cw[CONV_WIDTH - 1 - d:CONV_WIDTH - d, :]
    edge_sc[0:SUBLANES, :] = x[ts - SUBLANES:, :]

    xcb = xc.astype(BF16)
    r = jax.nn.sigmoid(jnp.dot(xcb, wa_ref[...], preferred_element_type=F32) + ba_ref[...])
    ig = jax.nn.sigmoid(jnp.dot(xcb, wx_ref[...], preferred_element_type=F32) + bx_ref[...])
    log_a = (-LRU_C) * r * _softplus(-lam_ref[...])
    a = jnp.exp(log_a)
    th = jnp.tanh(log_a)
    m2 = -2.0 * th / (1.0 - th)
    mult = jnp.where(m2 > 0.0, m2 * lax.rsqrt(m2), 0.0)
    gated = ig * xc
    b = mult * gated
    row8 = lax.broadcasted_iota(jnp.int32, (SUBLANES, LANES), 0)
    first = jnp.where((row8 == 0) & (tb == 0), gated[0:SUBLANES, :], b[0:SUBLANES, :])
    b = jnp.concatenate([first, b[SUBLANES:, :]], axis=0)

    d = 1
    while d < ts:
        if d < SUBLANES:
            a_sc[SUBLANES:, :] = a
            b_sc[SUBLANES:, :] = b
            a_sh = a_sc[SUBLANES - d:SUBLANES - d + ts, :]
            b_sh = b_sc[SUBLANES - d:SUBLANES - d + ts, :]
            b = a * b_sh + b
            a = a * a_sh
        else:
            a_sh = jnp.concatenate([jnp.ones((d, LANES), F32), a[:ts - d]], axis=0)
            b_sh = jnp.concatenate([jnp.zeros((d, LANES), F32), b[:ts - d]], axis=0)
            b = a * b_sh + b
            a = a * a_sh
        d *= 2
    h = a * h_sc[0:1, :] + b
    h_sc[...] = jnp.broadcast_to(h[ts - 1:ts, :], h_sc.shape)
    y_ref[...] = (h * gy_ref[...].astype(F32)).astype(y_ref.dtype)


def rglru(x_r, gy, conv_w, conv_b, w_rg_a, b_rg_a, w_rg_x, b_rg_x, lru_lambda, *, ts=512):
    s, c = x_r.shape
    ts = min(ts, s)
    nb = c // LANES
    col = lambda cb, tb: (0, cb)
    blk = lambda cb, tb: (cb, 0, 0)
    return pl.pallas_call(
        functools.partial(_rglru_kernel, ts=ts),
        grid=(nb, s // ts),
        in_specs=[
            pl.BlockSpec((ts, LANES), lambda cb, tb: (tb, cb)),
            pl.BlockSpec((ts, LANES), lambda cb, tb: (tb, cb)),
            pl.BlockSpec((CONV_WIDTH, LANES), col),
            pl.BlockSpec((1, LANES), col),
            pl.BlockSpec((None, RNN_BLOCK_DIM, RNN_BLOCK_DIM), blk),
            pl.BlockSpec((None, 1, RNN_BLOCK_DIM), blk),
            pl.BlockSpec((None, RNN_BLOCK_DIM, RNN_BLOCK_DIM), blk),
            pl.BlockSpec((None, 1, RNN_BLOCK_DIM), blk),
            pl.BlockSpec((1, LANES), col),
        ],
        out_specs=pl.BlockSpec((ts, LANES), lambda cb, tb: (tb, cb)),
        out_shape=jax.ShapeDtypeStruct((s, c), BF16),
        scratch_shapes=[pltpu.VMEM((SUBLANES, LANES), F32), pltpu.VMEM((2 * SUBLANES, LANES), F32),
                        pltpu.VMEM((SUBLANES + ts, LANES), F32), pltpu.VMEM((SUBLANES + ts, LANES), F32)],
        compiler_params=_cparams("parallel", "arbitrary"),
        name="rglru",
    )(x_r, gy, conv_w, conv_b.reshape(1, c), w_rg_a.astype(BF16), b_rg_a.reshape(nb, 1, RNN_BLOCK_DIM),
      w_rg_x.astype(BF16), b_rg_x.reshape(nb, 1, RNN_BLOCK_DIM), lru_lambda.reshape(1, c))


def _sorting_network(n):
    pairs = []

    def merge(lo, length, r):
        step = r * 2
        if step < length:
            merge(lo, length, step)
            merge(lo + r, length, step)
            pairs.extend((i, i + r) for i in range(lo + r, lo + length - r, step))
        else:
            pairs.append((lo, lo + r))

    def sort(lo, length):
        if length > 1:
            half = length // 2
            sort(lo, half)
            sort(lo + half, half)
            merge(lo, length, 1)

    sort(0, n)
    return pairs


def _topk_keys(s, k, vals_ref, idx_ref):
    n_slabs = s.shape[0] // SUBLANES
    assert n_slabs == k
    sub = lax.broadcasted_iota(jnp.int32, (SUBLANES, s.shape[1]), 0)
    vals = [s[v * SUBLANES:(v + 1) * SUBLANES, :] for v in range(n_slabs)]
    idxs = [sub + v * SUBLANES for v in range(n_slabs)]
    for i, j in _sorting_network(n_slabs):
        first = (vals[i] > vals[j]) | ((vals[i] == vals[j]) & (idxs[i] < idxs[j]))
        vals[i], vals[j] = jnp.where(first, vals[i], vals[j]), jnp.where(first, vals[j], vals[i])
        idxs[i], idxs[j] = jnp.where(first, idxs[i], idxs[j]), jnp.where(first, idxs[j], idxs[i])
    for r in range(k):
        m = jnp.max(vals[0], axis=0, keepdims=True)
        im = jnp.min(jnp.where(vals[0] == m, idxs[0], N_KEYS), axis=0, keepdims=True)
        vals_ref[r:r + 1, :] = m
        idx_ref[r:r + 1, :] = im
        taken = idxs[0] == im
        for v in range(k - 1 - r):
            vals[v] = jnp.where(taken, vals[v + 1], vals[v])
            idxs[v] = jnp.where(taken, idxs[v + 1], idxs[v])


def _pair_candidates(v0_sc, v1_sc):
    k = PEER_TOPK
    tt = v0_sc.shape[1]
    row8 = lax.broadcasted_iota(jnp.int32, (SUBLANES, tt), 0)
    v0_lo = v0_sc[0:SUBLANES, :]
    vals = [v0_lo + v1_sc[0:1, :], v0_sc[SUBLANES:k, :] + v1_sc[0:1, :]]
    flat = [row8 * k, (row8 + SUBLANES) * k]
    for b in range(1, SUBLANES):
        ok = row8 < k // (b + 1)
        vals.append(jnp.where(ok, v0_lo + v1_sc[b:b + 1, :], -jnp.inf))
        flat.append(jnp.where(ok, row8 * k + b, k * k))
    vals.append(v0_sc[0:1, :] + v1_sc[SUBLANES:k, :])
    flat.append(row8 + SUBLANES)
    return jnp.concatenate(vals, axis=0), jnp.concatenate(flat, axis=0)


def _peer_route_kernel(pq_ref, sk_ref, i1_ref, i2_ref, g_ref, v0_sc, i0_sc, v1_sc, j1_sc, bs_sc, bc_sc):
    k = PEER_TOPK
    for c, (v_sc, i_sc) in enumerate(((v0_sc, i0_sc), (v1_sc, j1_sc))):
        s_t = lax.dot_general(sk_ref[c], pq_ref[c].astype(BF16), NT_DIMS, preferred_element_type=F32)
        _topk_keys(s_t, k, v_sc, i_sc)
    cand, flat = _pair_candidates(v0_sc, v1_sc)
    for r in range(k):
        m = jnp.max(cand, axis=0, keepdims=True)
        am = jnp.min(jnp.where(cand == m, flat, k * k), axis=0, keepdims=True)
        bs_sc[r:r + 1, :] = m
        bc_sc[r:r + 1, :] = am
        cand = jnp.where(flat == am, -jnp.inf, cand)
    bs = bs_sc[...]
    bc = bc_sc[...]
    row = lax.broadcasted_iota(jnp.int32, bs.shape, 0)
    i0 = i0_sc[...]
    j1 = j1_sc[...]
    for r in range(k):
        ca = bc[r:r + 1, :] >> 4
        cb = bc[r:r + 1, :] & (k - 1)
        i1_ref[r:r + 1, :] = jnp.sum(jnp.where(row == ca, i0, 0), axis=0, keepdims=True)
        i2_ref[r:r + 1, :] = jnp.sum(jnp.where(row == cb, j1, 0), axis=0, keepdims=True)
    e = jnp.exp(bs - bs[0:1, :])
    g_ref[...] = e / jnp.sum(e, axis=0, keepdims=True)


def peer_route(pq, subkeys, *, tt=256):
    s = pq.shape[1]
    tt = min(tt, s)
    k = PEER_TOPK
    out = jax.ShapeDtypeStruct((PEER_HEADS * k, s), jnp.int32)
    ospec = pl.BlockSpec((k, tt), lambda i, h: (h, i))
    return pl.pallas_call(
        _peer_route_kernel,
        grid=(s // tt, PEER_HEADS),
        in_specs=[pl.BlockSpec((2, tt, PEER_HALF), lambda i, h: (h, i, 0)),
                  pl.BlockSpec((2, N_KEYS, PEER_HALF), lambda i, h: (h, 0, 0))],
        out_specs=[ospec, ospec, ospec],
        out_shape=[out, out, jax.ShapeDtypeStruct((PEER_HEADS * k, s), F32)],
        scratch_shapes=[pltpu.VMEM((k, tt), F32), pltpu.VMEM((k, tt), jnp.int32),
                        pltpu.VMEM((k, tt), F32), pltpu.VMEM((k, tt), jnp.int32),
                        pltpu.VMEM((k, tt), F32), pltpu.VMEM((k, tt), jnp.int32)],
        compiler_params=_cparams("parallel", "parallel"),
        name="peer_route",
    )(pq, subkeys)


TABLE_GROUP = 16
TABLE_PITCH = N_KEYS + SUBLANES
TABLE_GROUPS_PER_STEP = 4


def _peer_table_kernel(i1t_ref, i2t_ref, gt_ref, w_ref, i1_sc, i2_sc, g_sc, stage_sc, *, tw):
    i1_sc[...] = i1t_ref[...].T
    i2_sc[...] = i2t_ref[...].T
    g_sc[...] = gt_ref[...].T
    sub = lax.broadcasted_iota(jnp.int32, (N_KEYS, PEER_HEADS * PEER_TOPK), 0)

    def groups(gi, carry):
        for sg in range(TABLE_GROUPS_PER_STEP):
            t0 = pl.multiple_of((gi * TABLE_GROUPS_PER_STEP + sg) * TABLE_GROUP, TABLE_GROUP)
            base = sg * TABLE_GROUP * TABLE_PITCH
            for u in range(TABLE_GROUP):
                r1 = i1_sc[pl.ds(t0 + u, 1), :]
                r2 = i2_sc[pl.ds(t0 + u, 1), :]
                gg = g_sc[pl.ds(t0 + u, 1), :]
                a = jnp.where(sub == r1, gg, 0.0).astype(BF16)
                b = jnp.where(sub == r2, 1.0, 0.0).astype(BF16)
                row0 = base + u * TABLE_PITCH
                stage_sc[row0:row0 + N_KEYS, :] = lax.dot_general(
                    a, b, NT_DIMS, preferred_element_type=F32)
            for k1 in range(N_KEYS):
                blk = stage_sc[pl.ds(base + k1, TABLE_GROUP, stride=TABLE_PITCH), :]
                w_ref[pl.ds(t0, TABLE_GROUP), k1 * N_KEYS:(k1 + 1) * N_KEYS] = blk.astype(w_ref.dtype)
        return carry

    lax.fori_loop(0, tw // (TABLE_GROUP * TABLE_GROUPS_PER_STEP), groups, 0)


def peer_table(i1t, i2t, gt, *, tw=256):
    hk, s = i1t.shape
    tw = min(tw, s)
    assert tw % (TABLE_GROUP * TABLE_GROUPS_PER_STEP) == 0
    ispec = pl.BlockSpec((hk, tw), lambda i: (0, i))
    return pl.pallas_call(
        functools.partial(_peer_table_kernel, tw=tw),
        grid=(s // tw,),
        in_specs=[ispec, ispec, ispec],
        out_specs=pl.BlockSpec((tw, N_KEYS * N_KEYS), lambda i: (i, 0)),
        out_shape=jax.ShapeDtypeStruct((s, N_KEYS * N_KEYS), BF16),
        scratch_shapes=[pltpu.VMEM((tw, hk), jnp.int32), pltpu.VMEM((tw, hk), jnp.int32),
                        pltpu.VMEM((tw, hk), F32),
                        pltpu.VMEM((TABLE_GROUPS_PER_STEP * TABLE_GROUP * TABLE_PITCH, N_KEYS), F32)],
        compiler_params=_cparams("parallel"),
        name="peer_table",
    )(i1t, i2t, gt)


def _peer_experts_kernel(n_ref, w_ref, u_ref, v_ref, o_ref):
    j = pl.program_id(1)

    @pl.when(j == 0)
    def _():
        o_ref[...] = jnp.zeros_like(o_ref)

    act = lax.dot_general(n_ref[...], u_ref[...].astype(BF16), NT_DIMS, preferred_element_type=F32)
    m = (jax.nn.gelu(act) * w_ref[...].astype(F32)).astype(BF16)
    o_ref[...] += jnp.dot(m, v_ref[...].astype(BF16), preferred_element_type=F32)


def peer_experts(n2, w_flat, u, v, *, tm=1024, te=1024):
    s, d = n2.shape
    e = u.shape[0]
    tm = min(tm, s)
    once = pl.Buffered(1)
    return pl.pallas_call(
        _peer_experts_kernel,
        grid=(s // tm, e // te),
        in_specs=[
            pl.BlockSpec((tm, d), lambda i, j: (i, 0), pipeline_mode=once),
            pl.BlockSpec((tm, te), lambda i, j: (i, j)),
            pl.BlockSpec((te, d), lambda i, j: (j, 0)),
            pl.BlockSpec((te, d), lambda i, j: (j, 0)),
        ],
        out_specs=pl.BlockSpec((tm, d), lambda i, j: (i, 0), pipeline_mode=once),
        out_shape=jax.ShapeDtypeStruct((s, d), F32),
        compiler_params=_cparams("parallel", "arbitrary"),
        name="peer_experts",
    )(n2, w_flat, u, v)


def _ple_kernel(h_ref, f_ref, p_ref, gple_ref, wg_ref, wp_ref, gfin_ref, o_ref, *, final):
    h = h_ref[...] + f_ref[...]
    n3 = _rms(h, gple_ref[...]).astype(BF16)
    gate = jax.nn.sigmoid(jnp.dot(n3, wg_ref[...], preferred_element_type=F32))
    pp = jnp.dot(p_ref[...], wp_ref[...], preferred_element_type=F32)
    h = h + gate * pp
    if final:
        h = _rms(h, gfin_ref[...])
    o_ref[...] = h


def ple_block(h1, ffn_out, p, ple_norm, w_gate, w_proj, final_norm, *, final, tm=256):
    s, d = h1.shape
    pd = p.shape[1]
    tm = min(tm, s)
    full = lambda i: (0, 0)
    return pl.pallas_call(
        functools.partial(_ple_kernel, final=final),
        grid=(s // tm,),
        in_specs=[
            pl.BlockSpec((tm, d), lambda i: (i, 0)),
            pl.BlockSpec((tm, d), lambda i: (i, 0)),
            pl.BlockSpec((tm, pd), lambda i: (i, 0)),
            pl.BlockSpec((1, d), full),
            pl.BlockSpec((d, d), full),
            pl.BlockSpec((pd, d), full),
            pl.BlockSpec((1, d), full),
        ],
        out_specs=pl.BlockSpec((tm, d), lambda i: (i, 0)),
        out_shape=jax.ShapeDtypeStruct((s, d), F32),
        compiler_params=_cparams("parallel"),
        name="ple_block",
    )(h1, ffn_out, p, ple_norm.reshape(1, d), w_gate, w_proj, final_norm.reshape(1, d))


def _rot_cols(w):
    half = QK_ROPE_DIM // 2
    return jnp.concatenate([-w[..., half:], w[..., :half]], axis=-1)


def _layer(h, p, cos128, sin128, prm, *, final, final_norm):
    s, d = h.shape
    (attn_norm, w_in, b_gate, q_norm, w_uq, kv_norm, w_ukv, w_attn_o, conv_w, conv_b, w_rg_a, b_rg_a,
     w_rg_x, b_rg_x, lru_lambda, w_rnn_o, w_out, ffn_norm, w_peer_q, peer_subkeys, peer_u, peer_v,
     ple_norm, w_ple_gate, w_ple_proj) = prm

    o_q, o_kv, o_kr, o_x, o_y, o_g = 0, Q_LORA_RANK, Q_LORA_RANK + KV_LORA_RANK, \
        Q_LORA_RANK + KV_LORA_RANK + QK_ROPE_DIM, Q_LORA_RANK + KV_LORA_RANK + QK_ROPE_DIM + d, \
        Q_LORA_RANK + KV_LORA_RANK + QK_ROPE_DIM + 2 * d
    w_in_t = w_in.T
    w_kr = w_in_t[o_kr:o_x]
    w_kr_rot = _rot_cols(w_kr.T).T
    w_lat = jnp.concatenate([w_in_t[o_q:o_kr], w_kr, w_kr, w_kr_rot, w_kr_rot], axis=0).astype(BF16)
    w_x = w_in_t[o_x:o_y].astype(BF16)
    w_y = w_in_t[o_y:o_g].astype(BF16)
    w_g = w_in_t[o_g:].astype(BF16)
    uq = w_uq.reshape(Q_LORA_RANK, N_HEADS, QK_HEAD_DIM)
    uq_pe = uq[:, :, QK_NOPE_DIM:]
    w_q = jnp.concatenate([uq[:, :, :QK_NOPE_DIM].reshape(Q_LORA_RANK, -1),
                           uq_pe.reshape(Q_LORA_RANK, -1),
                           _rot_cols(uq_pe).reshape(Q_LORA_RANK, -1)], axis=1).astype(BF16)
    n_lat = Q_LORA_RANK + KV_LORA_RANK + 2 * LANES
    n_qn = N_HEADS * QK_NOPE_DIM
    n_qp = N_HEADS * QK_ROPE_DIM
    scale = QK_HEAD_DIM ** -0.5 * math.log2(math.e)

    n1 = rmsnorm_bf16(h, attn_norm)

    def lat_body(acc, extra, outs):
        outs[0][...] = acc[:, :Q_LORA_RANK]
        outs[1][...] = acc[:, Q_LORA_RANK:Q_LORA_RANK + KV_LORA_RANK]
        outs[2][...] = acc[:, Q_LORA_RANK + KV_LORA_RANK:]

    tm_lat = 512
    c_q, c_kv, kr4 = _mm_call(
        "in_proj_latents", n1, w_lat, tm=tm_lat, tn=n_lat, body=lat_body, w_rows_are_outputs=True,
        out_shapes=[jax.ShapeDtypeStruct((s, Q_LORA_RANK), F32), jax.ShapeDtypeStruct((s, KV_LORA_RANK), F32),
                    jax.ShapeDtypeStruct((s, 2 * LANES), F32)],
        out_specs=[pl.BlockSpec((min(tm_lat, s), Q_LORA_RANK), lambda j, i: (i, 0)),
                   pl.BlockSpec((min(tm_lat, s), KV_LORA_RANK), lambda j, i: (i, 0)),
                   pl.BlockSpec((min(tm_lat, s), 2 * LANES), lambda j, i: (i, 0))])

    tm, tn = min(512, s), 1024

    def store_body(acc, extra, outs):
        outs[0][...] = acc.astype(outs[0].dtype)

    x_r = _mm_call("in_proj_x", n1, w_x, tm=tm, tn=tn, body=store_body, w_rows_are_outputs=True,
                   out_shapes=jax.ShapeDtypeStruct((s, d), F32), out_specs=_tile_spec(tm, tn))

    def gelu_body(acc, extra, outs):
        outs[0][...] = jax.nn.gelu(acc).astype(outs[0].dtype)

    gy = _mm_call("in_proj_y", n1, w_y, tm=tm, tn=tn, body=gelu_body, w_rows_are_outputs=True,
                  out_shapes=jax.ShapeDtypeStruct((s, d), BF16), out_specs=_tile_spec(tm, tn))

    def gate_body(acc, extra, outs):
        outs[0][...] = jax.nn.sigmoid(acc + extra[0][...]).astype(outs[0].dtype)

    gates = _mm_call("in_proj_gates", n1, w_g, tm=tm, tn=tn, body=gate_body, w_rows_are_outputs=True,
                     extras=(b_gate.reshape(1, 2 * d),), extra_specs=(_col_spec(tn),),
                     out_shapes=jax.ShapeDtypeStruct((s, 2 * d), BF16), out_specs=_tile_spec(tm, tn))

    def norm_prologue(a, extra):
        return _rms(a, extra[0][...]).astype(BF16)

    def q_body(acc, extra, outs):
        outs[0][...] = (acc[:, :n_qn] * scale).astype(BF16)
        cos_t = jnp.tile(extra[1][...], (1, n_qp // LANES))
        sin_t = jnp.tile(extra[2][...], (1, n_qp // LANES))
        pe = acc[:, n_qn:n_qn + n_qp] * cos_t + acc[:, n_qn + n_qp:] * sin_t
        outs[1][...] = (pe * scale).astype(BF16)

    tm_q = min(256, s)
    row128 = pl.BlockSpec((tm_q, LANES), lambda j, i: (i, 0))
    qn, qpe = _mm_call(
        "q_proj", c_q, w_q, tm=tm_q, tn=n_qn + 2 * n_qp, prologue=norm_prologue, body=q_body,
        extras=(q_norm.reshape(1, Q_LORA_RANK), cos128, sin128),
        extra_specs=(pl.BlockSpec((1, Q_LORA_RANK), lambda j, i: (0, 0)), row128, row128),
        out_shapes=[jax.ShapeDtypeStruct((s, n_qn), BF16), jax.ShapeDtypeStruct((s, n_qp), BF16)],
        out_specs=[pl.BlockSpec((tm_q, n_qn), lambda j, i: (i, 0)), pl.BlockSpec((tm_q, n_qp), lambda j, i: (i, 0))])

    def kv_body(acc, extra, outs):
        outs[0][...] = acc.astype(BF16)
        kr = extra[1][...]
        kpe2 = kr[:, :LANES] * extra[2][...] + kr[:, LANES:] * extra[3][...]
        lane = lax.broadcasted_iota(jnp.int32, kpe2.shape, 1)
        outs[1][...] = jnp.concatenate([jnp.where(lane < QK_ROPE_DIM, kpe2, 0.0),
                                        jnp.where(lane >= QK_ROPE_DIM, kpe2, 0.0)], axis=1).astype(BF16)

    n_kv = N_HEADS * (QK_NOPE_DIM + V_HEAD_DIM)
    kv, kpe = _mm_call(
        "kv_proj", c_kv, w_ukv.astype(BF16), tm=tm_q, tn=n_kv, prologue=norm_prologue, body=kv_body,
        extras=(kv_norm.reshape(1, KV_LORA_RANK), kr4, cos128, sin128),
        extra_specs=(pl.BlockSpec((1, KV_LORA_RANK), lambda j, i: (0, 0)),
                     pl.BlockSpec((tm_q, 2 * LANES), lambda j, i: (i, 0)), row128, row128),
        out_shapes=[jax.ShapeDtypeStruct((s, n_kv), BF16), jax.ShapeDtypeStruct((s, 2 * LANES), BF16)],
        out_specs=[pl.BlockSpec((tm_q, n_kv), lambda j, i: (i, 0)), pl.BlockSpec((tm_q, 2 * LANES), lambda j, i: (i, 0))])

    o_attn = mla_attention(qn, qpe, kv, kpe)
    y_rnn_in = rglru(x_r, gy, conv_w, conv_b, w_rg_a, b_rg_a, w_rg_x, b_rg_x, lru_lambda)

    def merge_kernel(o_ref, y_ref, wo_ref, wr_ref, ga_ref, gr_ref, m_ref):
        ya = jnp.dot(o_ref[...], wo_ref[...], preferred_element_type=F32)
        yr = jnp.dot(y_ref[...], wr_ref[...], preferred_element_type=F32)
        m_ref[...] = (ga_ref[...].astype(F32) * ya + gr_ref[...].astype(F32) * yr).astype(m_ref.dtype)

    a_spec = pl.BlockSpec((tm, d), lambda j, i: (i, 0))
    w_spec = pl.BlockSpec((d, tn), lambda j, i: (0, j))
    merged = pl.pallas_call(
        merge_kernel,
        grid=(d // tn, s // tm),
        in_specs=[a_spec, a_spec, w_spec, w_spec, _tile_spec(tm, tn), _tile_spec(tm, tn, d // tn)],
        out_specs=_tile_spec(tm, tn),
        out_shape=jax.ShapeDtypeStruct((s, d), BF16),
        compiler_params=_cparams("parallel", "parallel"),
        name="mixer_merge",
    )(o_attn, y_rnn_in, w_attn_o.astype(BF16), w_rnn_o.astype(BF16), gates, gates)

    def out_body(acc, extra, outs):
        h1 = extra[0][...] + acc
        outs[0][...] = h1
        outs[1][...] = _rms(h1, extra[1][...]).astype(BF16)

    tm_o = min(256, s)
    rowd = pl.BlockSpec((tm_o, d), lambda j, i: (i, 0))
    h1, n2 = _mm_call(
        "out_proj", merged, w_out.astype(BF16), tm=tm_o, tn=d, body=out_body,
        extras=(h, ffn_norm.reshape(1, d)), extra_specs=(rowd, pl.BlockSpec((1, d), lambda j, i: (0, 0))),
        out_shapes=[jax.ShapeDtypeStruct((s, d), F32), jax.ShapeDtypeStruct((s, d), BF16)],
        out_specs=[rowd, rowd])

    n_hc = 2 * PEER_HEADS

    def pq_body(acc, extra, outs):
        for c in range(n_hc):
            outs[0][c] = acc[:, c * PEER_HALF:(c + 1) * PEER_HALF]

    pq = _mm_call("peer_query", n2, w_peer_q.astype(BF16), tm=tm_o, tn=n_hc * PEER_HALF, body=pq_body,
                  out_shapes=jax.ShapeDtypeStruct((n_hc, s, PEER_HALF), F32),
                  out_specs=pl.BlockSpec((n_hc, tm_o, PEER_HALF), lambda j, i: (0, i, 0)))
    i1t, i2t, gt = peer_route(pq, peer_subkeys.reshape(n_hc, N_KEYS, PEER_HALF).astype(BF16))
    w_tab = peer_table(i1t, i2t, gt)
    ffn_out = peer_experts(n2, w_tab, peer_u, peer_v)

    return ple_block(h1, ffn_out, p.astype(BF16), ple_norm, w_ple_gate.astype(BF16), w_ple_proj.astype(BF16),
                     final_norm, final=final)


def kernel(x, p, attn_norm, w_in, b_gate, q_norm, w_uq, kv_norm, w_ukv, w_attn_o, conv_w, conv_b, w_rg_a, b_rg_a,
           w_rg_x, b_rg_x, lru_lambda, w_rnn_o, w_out, ffn_norm, w_peer_q, peer_subkeys, peer_u, peer_v, ple_norm,
           w_ple_gate, w_ple_proj, final_norm):
    bsz, s, d = x.shape
    assert bsz == 1, "one sequence per call"
    depth = w_in.shape[0]
    layer_params = (attn_norm, w_in, b_gate, q_norm, w_uq, kv_norm, w_ukv, w_attn_o, conv_w, conv_b, w_rg_a,
                    b_rg_a, w_rg_x, b_rg_x, lru_lambda, w_rnn_o, w_out, ffn_norm, w_peer_q, peer_subkeys, peer_u,
                    peer_v, ple_norm, w_ple_gate, w_ple_proj)

    half = QK_ROPE_DIM // 2
    inv_freq = 1.0 / (ROPE_THETA ** (jnp.arange(half, dtype=F32) / half))
    ang = jnp.arange(s, dtype=F32)[:, None] * inv_freq[None, :]
    cos128 = jnp.tile(jnp.cos(ang), (1, LANES // half))
    sin128 = jnp.tile(jnp.sin(ang), (1, LANES // half))

    h = x.reshape(s, d)
    for l in range(depth):
        h = _layer(h, p[l, 0], cos128, sin128, tuple(w[l] for w in layer_params),
                   final=(l == depth - 1), final_norm=final_norm)
    return h.reshape(bsz, s, d)
```

```python
import functools
import math

import jax
import jax.numpy as jnp
from jax import lax
from jax.experimental import pallas as pl
from jax.experimental.pallas import tpu as pltpu

RMS_EPS = 1e-6
N_HEADS = 16
QK_NOPE_DIM = 128
QK_ROPE_DIM = 64
V_HEAD_DIM = 128
QK_HEAD_DIM = QK_NOPE_DIM + QK_ROPE_DIM
Q_LORA_RANK = 768
KV_LORA_RANK = 512
ROPE_THETA = 10000.0
RNN_BLOCKS = 16
RNN_BLOCK_DIM = 128
CONV_WIDTH = 4
LRU_C = 8.0
PEER_HEADS = 8
N_KEYS = 128
PEER_HALF = 128
PEER_TOPK = 16

LANES = 128
SUBLANES = 8
VMEM_LIMIT_BYTES = 56 * 1024 * 1024

F32 = jnp.float32
BF16 = jnp.bfloat16
NT_DIMS = (((1,), (1,)), ((), ()))


def _cparams(*sem):
    return pltpu.CompilerParams(dimension_semantics=sem, vmem_limit_bytes=VMEM_LIMIT_BYTES)


def _rms(x, g):
    ms = jnp.mean(x * x, axis=-1, keepdims=True)
    return (x * lax.rsqrt(ms + RMS_EPS)) * g


def _rmsnorm_kernel(x_ref, g_ref, o_ref):
    o_ref[...] = _rms(x_ref[...], g_ref[...]).astype(o_ref.dtype)


def rmsnorm_bf16(x, g, *, tr=512):
    m, d = x.shape
    tr = min(tr, m)
    return pl.pallas_call(
        _rmsnorm_kernel,
        grid=(m // tr,),
        in_specs=[pl.BlockSpec((tr, d), lambda i: (i, 0)), pl.BlockSpec((1, d), lambda i: (0, 0))],
        out_specs=pl.BlockSpec((tr, d), lambda i: (i, 0)),
        out_shape=jax.ShapeDtypeStruct((m, d), BF16),
        compiler_params=_cparams("parallel"),
        name="rmsnorm",
    )(x, g.reshape(1, d))


def _mm_kernel(a_ref, w_ref, *refs, n_extra, prologue, body, w_rows_are_outputs):
    extra = refs[:n_extra]
    outs = refs[n_extra:]
    a = a_ref[...]
    if prologue is not None:
        a = prologue(a, extra)
    if w_rows_are_outputs:
        acc = lax.dot_general(a, w_ref[...], NT_DIMS, preferred_element_type=F32)
    else:
        acc = jnp.dot(a, w_ref[...], preferred_element_type=F32)
    body(acc, extra, outs)


def _mm_call(name, a, w, *, tm, tn, extras=(), extra_specs=(), out_shapes, out_specs, body, prologue=None,
             w_rows_are_outputs=False):
    m, k = a.shape
    n = w.shape[0] if w_rows_are_outputs else w.shape[1]
    tm = min(tm, m)
    assert m % tm == 0 and n % tn == 0
    kern = functools.partial(_mm_kernel, n_extra=len(extras), prologue=prologue, body=body,
                             w_rows_are_outputs=w_rows_are_outputs)
    w_spec = pl.BlockSpec((tn, k), lambda j, i: (j, 0)) if w_rows_are_outputs else \
        pl.BlockSpec((k, tn), lambda j, i: (0, j))
    return pl.pallas_call(
        kern,
        grid=(n // tn, m // tm),
        in_specs=[pl.BlockSpec((tm, k), lambda j, i: (i, 0)), w_spec]
        + list(extra_specs),
        out_specs=out_specs,
        out_shape=out_shapes,
        compiler_params=_cparams("parallel", "parallel"),
        name=name,
    )(a, w, *extras)


def _tile_spec(tm, tn, col_off_blocks=0):
    return pl.BlockSpec((tm, tn), lambda j, i: (i, j + col_off_blocks))


def _col_spec(tn):
    return pl.BlockSpec((1, tn), lambda j, i: (0, j))


ATTN_ROW_CHUNK = 32


def _attn_kernel(qn_ref, qpe_ref, kv_ref, kpe_ref, o_ref, s0_sc, s1_sc, p0_sc, p1_sc, a0_sc, a1_sc,
                 m_sc, l_sc, acc_sc, *, tq):
    i = pl.program_id(1)
    rc = ATTN_ROW_CHUNK
    qpe = qpe_ref[...]
    qs = [jnp.concatenate([qn_ref[:, hh * LANES:(hh + 1) * LANES], qpe], axis=1) for hh in range(2)]
    m_sc[...] = jnp.full(m_sc.shape, -jnp.inf, F32)
    l_sc[...] = jnp.zeros(l_sc.shape, F32)
    acc_sc[...] = jnp.zeros(acc_sc.shape, F32)
    chunks = [slice(c * rc, (c + 1) * rc) for c in range(tq // rc)]
    even = (s0_sc, p0_sc, a0_sc)
    odd = (s1_sc, p1_sc, a1_sc)

    def scores(kb, bufs):
        s_sc = bufs[0]
        start = pl.multiple_of(kb * tq, tq)
        for hh in range(2):
            k = jnp.concatenate(
                [kv_ref[pl.ds(start, tq), hh * 256:hh * 256 + LANES],
                 kpe_ref[pl.ds(start, tq), hh * LANES:(hh + 1) * LANES]], axis=1)
            s_sc[hh] = lax.dot_general(qs[hh], k, NT_DIMS, preferred_element_type=F32)

    def mask_diagonal(bufs):
        s_sc = bufs[0]
        for hh in range(2):
            for rows in chunks:
                row = lax.broadcasted_iota(jnp.int32, (rc, tq), 0) + rows.start
                col = lax.broadcasted_iota(jnp.int32, (rc, tq), 1)
                s_sc[hh, rows, :] = jnp.where(col <= row, s_sc[hh, rows, :], -jnp.inf)

    def accumulate(kb, bufs):
        s_sc, p_sc, a_sc = bufs
        start = pl.multiple_of(kb * tq, tq)
        for hh in range(2):
            for rows in chunks:
                m_old = m_sc[hh, rows, :]
                m_new = jnp.maximum(m_old, jnp.max(s_sc[hh, rows, :], axis=-1, keepdims=True))
                a_sc[hh, rows, :] = jnp.exp2(m_old - m_new)
                m_sc[hh, rows, :] = m_new
            for rows in chunks:
                m_new = jnp.concatenate([m_sc[hh, rows, :]] * (tq // LANES), axis=1)
                p = jnp.exp2(s_sc[hh, rows, :] - m_new)
                l_sc[hh, rows, :] = a_sc[hh, rows, :] * l_sc[hh, rows, :] + jnp.sum(p, axis=-1, keepdims=True)
                p_sc[hh, rows, :] = p.astype(BF16)
        for hh in range(2):
            v = kv_ref[pl.ds(start, tq), hh * 256 + LANES:(hh + 1) * 256]
            acc_sc[hh] = a_sc[hh] * acc_sc[hh] + jnp.dot(p_sc[hh], v, preferred_element_type=F32)

    scores(0, even)

    def two_blocks(kk, carry):
        kb = 2 * kk + 1
        scores(kb, odd)
        accumulate(kb - 1, even)
        scores(kb + 1, even)
        accumulate(kb, odd)
        return carry

    lax.fori_loop(0, i // 2, two_blocks, 0)

    @pl.when(i % 2 == 1)
    def _():
        scores(i, odd)
        accumulate(i - 1, even)
        mask_diagonal(odd)
        accumulate(i, odd)

    @pl.when(i % 2 == 0)
    def _():
        mask_diagonal(even)
        accumulate(i, even)

    for hh in range(2):
        o_ref[:, hh * LANES:(hh + 1) * LANES] = (acc_sc[hh] / l_sc[hh]).astype(o_ref.dtype)


def mla_attention(qn, qpe, kv, kpe, *, tq=512):
    s = qn.shape[0]
    tq = min(tq, s)
    n_pairs = N_HEADS // 2
    return pl.pallas_call(
        functools.partial(_attn_kernel, tq=tq),
        grid=(n_pairs, s // tq),
        in_specs=[
            pl.BlockSpec((tq, 2 * LANES), lambda j, i: (i, j)),
            pl.BlockSpec((tq, LANES), lambda j, i: (i, j)),
            pl.BlockSpec((s, 4 * LANES), lambda j, i: (0, j)),
            pl.BlockSpec((s, 2 * LANES), lambda j, i: (0, 0)),
        ],
        out_specs=pl.BlockSpec((tq, 2 * LANES), lambda j, i: (i, j)),
        out_shape=jax.ShapeDtypeStruct((s, N_HEADS * V_HEAD_DIM), BF16),
        scratch_shapes=[pltpu.VMEM((2, tq, tq), F32), pltpu.VMEM((2, tq, tq), F32),
                        pltpu.VMEM((2, tq, tq), BF16), pltpu.VMEM((2, tq, tq), BF16),
                        pltpu.VMEM((2, tq, LANES), F32), pltpu.VMEM((2, tq, LANES), F32),
                        pltpu.VMEM((2, tq, LANES), F32), pltpu.VMEM((2, tq, LANES), F32),
                        pltpu.VMEM((2, tq, V_HEAD_DIM), F32)],
        compiler_params=_cparams("parallel", "parallel"),
        name="mla_attention",
    )(qn, qpe, kv, kpe)


def _softplus(z):
    return jnp.maximum(z, 0.0) + jnp.log1p(jnp.exp(-jnp.abs(z)))


RGLRU_BLOCKS_PER_STEP = 4


def _rglru_kernel(x_ref, gy_ref, cw_ref, cb_ref, wa_ref, ba_ref, wx_ref, bx_ref, lam_ref, y_ref,
                  h_sc, edge_sc, a_sc, b_sc, *, ts):
    tb = pl.program_id(1)

    @pl.when(tb == 0)
    def _():
        h_sc[...] = jnp.zeros_like(h_sc)
        edge_sc[0:SUBLANES, :] = jnp.zeros((SUBLANES, edge_sc.shape[1]), F32)
        a_sc[0:SUBLANES, :] = jnp.ones((SUBLANES, a_sc.shape[1]), F32)
        b_sc[0:SUBLANES, :] = jnp.zeros((SUBLANES, b_sc.shape[1]), F32)

    row8 = lax.broadcasted_iota(jnp.int32, (SUBLANES, LANES), 0)
    for c in range(RGLRU_BLOCKS_PER_STEP):
        ln = slice(c * LANES, (c + 1) * LANES)
        x = x_ref[:, ln]
        cw = cw_ref[:, ln]
        edge_sc[SUBLANES:2 * SUBLANES, ln] = x[0:SUBLANES, :]
        xc = x * cw[CONV_WIDTH - 1:CONV_WIDTH, :] + cb_ref[:, ln]
        for d in range(1, CONV_WIDTH):
            xs = jnp.concatenate([edge_sc[SUBLANES - d:2 * SUBLANES - d, ln], x_ref[SUBLANES - d:ts - d, ln]], axis=0)
            xc = xc + xs * cw[CONV_WIDTH - 1 - d:CONV_WIDTH - d, :]
        edge_sc[0:SUBLANES, ln] = x[ts - SUBLANES:, :]

        xcb = xc.astype(BF16)
        r = jax.nn.sigmoid(jnp.dot(xcb, wa_ref[c], preferred_element_type=F32) + ba_ref[c])
        ig = jax.nn.sigmoid(jnp.dot(xcb, wx_ref[c], preferred_element_type=F32) + bx_ref[c])
        log_a = (-LRU_C) * r * _softplus(-lam_ref[:, ln])
        a = jnp.exp(log_a)
        th = jnp.tanh(log_a)
        m2 = -2.0 * th / (1.0 - th)
        mult = jnp.where(m2 > 0.0, m2 * lax.rsqrt(m2), 0.0)
        gated = ig * xc
        b = mult * gated
        first = jnp.where((row8 == 0) & (tb == 0), gated[0:SUBLANES, :], b[0:SUBLANES, :])
        b = jnp.concatenate([first, b[SUBLANES:, :]], axis=0)

        d = 1
        while d < ts:
            if d < SUBLANES:
                a_sc[SUBLANES:, ln] = a
                b_sc[SUBLANES:, ln] = b
                a_sh = a_sc[SUBLANES - d:SUBLANES - d + ts, ln]
                b_sh = b_sc[SUBLANES - d:SUBLANES - d + ts, ln]
            else:
                a_sh = jnp.concatenate([jnp.ones((d, LANES), F32), a[:ts - d]], axis=0)
                b_sh = jnp.concatenate([jnp.zeros((d, LANES), F32), b[:ts - d]], axis=0)
            b = a * b_sh + b
            a = a * a_sh
            d *= 2
        h = a * h_sc[0:1, ln] + b
        h_sc[:, ln] = jnp.broadcast_to(h[ts - 1:ts, :], (SUBLANES, LANES))
        y_ref[:, ln] = (h * gy_ref[:, ln].astype(F32)).astype(y_ref.dtype)


def rglru(x_r, gy, conv_w, conv_b, w_rg_a, b_rg_a, w_rg_x, b_rg_x, lru_lambda, *, ts=512):
    s, c = x_r.shape
    ts = min(ts, s)
    nblk = RGLRU_BLOCKS_PER_STEP
    wide = nblk * LANES
    n_gate_blocks = c // RNN_BLOCK_DIM
    col = lambda cb, tb: (0, cb)
    blk = lambda cb, tb: (cb, 0, 0)
    return pl.pallas_call(
        functools.partial(_rglru_kernel, ts=ts),
        grid=(c // wide, s // ts),
        in_specs=[
            pl.BlockSpec((ts, wide), lambda cb, tb: (tb, cb)),
            pl.BlockSpec((ts, wide), lambda cb, tb: (tb, cb)),
            pl.BlockSpec((CONV_WIDTH, wide), col),
            pl.BlockSpec((1, wide), col),
            pl.BlockSpec((nblk, RNN_BLOCK_DIM, RNN_BLOCK_DIM), blk),
            pl.BlockSpec((nblk, 1, RNN_BLOCK_DIM), blk),
            pl.BlockSpec((nblk, RNN_BLOCK_DIM, RNN_BLOCK_DIM), blk),
            pl.BlockSpec((nblk, 1, RNN_BLOCK_DIM), blk),
            pl.BlockSpec((1, wide), col),
        ],
        out_specs=pl.BlockSpec((ts, wide), lambda cb, tb: (tb, cb)),
        out_shape=jax.ShapeDtypeStruct((s, c), BF16),
        scratch_shapes=[pltpu.VMEM((SUBLANES, wide), F32), pltpu.VMEM((2 * SUBLANES, wide), F32),
                        pltpu.VMEM((SUBLANES + ts, wide), F32), pltpu.VMEM((SUBLANES + ts, wide), F32)],
        compiler_params=_cparams("parallel", "arbitrary"),
        name="rglru",
    )(x_r, gy, conv_w, conv_b.reshape(1, c), w_rg_a.astype(BF16), b_rg_a.reshape(n_gate_blocks, 1, RNN_BLOCK_DIM),
      w_rg_x.astype(BF16), b_rg_x.reshape(n_gate_blocks, 1, RNN_BLOCK_DIM), lru_lambda.reshape(1, c))


def _sorting_network(n):
    pairs = []

    def merge(lo, length, r):
        step = r * 2
        if step < length:
            merge(lo, length, step)
            merge(lo + r, length, step)
            pairs.extend((i, i + r) for i in range(lo + r, lo + length - r, step))
        else:
            pairs.append((lo, lo + r))

    def sort(lo, length):
        if length > 1:
            half = length // 2
            sort(lo, half)
            sort(lo + half, half)
            merge(lo, length, 1)

    sort(0, n)
    return pairs


def _topk_keys(s, k, vals_ref, idx_ref):
    n_slabs = s.shape[0] // SUBLANES
    assert n_slabs == k
    sub = lax.broadcasted_iota(jnp.int32, (SUBLANES, s.shape[1]), 0)
    vals = [s[v * SUBLANES:(v + 1) * SUBLANES, :] for v in range(n_slabs)]
    idxs = [sub + v * SUBLANES for v in range(n_slabs)]
    for i, j in _sorting_network(n_slabs):
        first = (vals[i] > vals[j]) | ((vals[i] == vals[j]) & (idxs[i] < idxs[j]))
        vals[i], vals[j] = jnp.where(first, vals[i], vals[j]), jnp.where(first, vals[j], vals[i])
        idxs[i], idxs[j] = jnp.where(first, idxs[i], idxs[j]), jnp.where(first, idxs[j], idxs[i])
    for r in range(k):
        m = jnp.max(vals[0], axis=0, keepdims=True)
        im = jnp.min(jnp.where(vals[0] == m, idxs[0], N_KEYS), axis=0, keepdims=True)
        vals_ref[r:r + 1, :] = m
        idx_ref[r:r + 1, :] = im
        taken = idxs[0] == im
        for v in range(k - 1 - r):
            vals[v] = jnp.where(taken, vals[v + 1], vals[v])
            idxs[v] = jnp.where(taken, idxs[v + 1], idxs[v])


def _pair_candidates(v0_sc, v1_sc):
    k = PEER_TOPK
    tt = v0_sc.shape[1]
    row8 = lax.broadcasted_iota(jnp.int32, (SUBLANES, tt), 0)
    v0_lo = v0_sc[0:SUBLANES, :]
    vals = [v0_lo + v1_sc[0:1, :], v0_sc[SUBLANES:k, :] + v1_sc[0:1, :]]
    flat = [row8 * k, (row8 + SUBLANES) * k]
    for b in range(1, SUBLANES):
        ok = row8 < k // (b + 1)
        vals.append(jnp.where(ok, v0_lo + v1_sc[b:b + 1, :], -jnp.inf))
        flat.append(jnp.where(ok, row8 * k + b, k * k))
    vals.append(v0_sc[0:1, :] + v1_sc[SUBLANES:k, :])
    flat.append(row8 + SUBLANES)
    return jnp.concatenate(vals, axis=0), jnp.concatenate(flat, axis=0)


def _peer_route_kernel(pq_ref, sk_ref, i1_ref, i2_ref, g_ref, v0_sc, i0_sc, v1_sc, j1_sc, bs_sc, bc_sc):
    k = PEER_TOPK
    for c, (v_sc, i_sc) in enumerate(((v0_sc, i0_sc), (v1_sc, j1_sc))):
        s_t = lax.dot_general(sk_ref[c], pq_ref[c].astype(BF16), NT_DIMS, preferred_element_type=F32)
        _topk_keys(s_t, k, v_sc, i_sc)
    cand, flat = _pair_candidates(v0_sc, v1_sc)
    for r in range(k):
        m = jnp.max(cand, axis=0, keepdims=True)
        am = jnp.min(jnp.where(cand == m, flat, k * k), axis=0, keepdims=True)
        bs_sc[r:r + 1, :] = m
        bc_sc[r:r + 1, :] = am
        cand = jnp.where(flat == am, -jnp.inf, cand)
    bs = bs_sc[...]
    bc = bc_sc[...]
    row = lax.broadcasted_iota(jnp.int32, bs.shape, 0)
    i0 = i0_sc[...]
    j1 = j1_sc[...]
    for r in range(k):
        ca = bc[r:r + 1, :] >> 4
        cb = bc[r:r + 1, :] & (k - 1)
        i1_ref[r:r + 1, :] = jnp.sum(jnp.where(row == ca, i0, 0), axis=0, keepdims=True)
        i2_ref[r:r + 1, :] = jnp.sum(jnp.where(row == cb, j1, 0), axis=0, keepdims=True)
    e = jnp.exp(bs - bs[0:1, :])
    g_ref[...] = e / jnp.sum(e, axis=0, keepdims=True)


def peer_route(pq, subkeys, *, tt=256):
    s = pq.shape[1]
    tt = min(tt, s)
    k = PEER_TOPK
    out = jax.ShapeDtypeStruct((PEER_HEADS * k, s), jnp.int32)
    ospec = pl.BlockSpec((k, tt), lambda i, h: (h, i))
    return pl.pallas_call(
        _peer_route_kernel,
        grid=(s // tt, PEER_HEADS),
        in_specs=[pl.BlockSpec((2, tt, PEER_HALF), lambda i, h: (h, i, 0)),
                  pl.BlockSpec((2, N_KEYS, PEER_HALF), lambda i, h: (h, 0, 0))],
        out_specs=[ospec, ospec, ospec],
        out_shape=[out, out, jax.ShapeDtypeStruct((PEER_HEADS * k, s), F32)],
        scratch_shapes=[pltpu.VMEM((k, tt), F32), pltpu.VMEM((k, tt), jnp.int32),
                        pltpu.VMEM((k, tt), F32), pltpu.VMEM((k, tt), jnp.int32),
                        pltpu.VMEM((k, tt), F32), pltpu.VMEM((k, tt), jnp.int32)],
        compiler_params=_cparams("parallel", "parallel"),
        name="peer_route",
    )(pq, subkeys)


TABLE_GROUP = 16
TABLE_PITCH = N_KEYS + SUBLANES
TABLE_GROUPS_PER_STEP = 4


def _peer_table_kernel(i1t_ref, i2t_ref, gt_ref, w_ref, i1_sc, i2_sc, g_sc, stage_sc, *, tw):
    i1_sc[...] = i1t_ref[...].T
    i2_sc[...] = i2t_ref[...].T
    g_sc[...] = gt_ref[...].T
    sub = lax.broadcasted_iota(jnp.int32, (N_KEYS, PEER_HEADS * PEER_TOPK), 0)

    def groups(gi, carry):
        for sg in range(TABLE_GROUPS_PER_STEP):
            t0 = pl.multiple_of((gi * TABLE_GROUPS_PER_STEP + sg) * TABLE_GROUP, TABLE_GROUP)
            base = sg * TABLE_GROUP * TABLE_PITCH
            for u in range(TABLE_GROUP):
                r1 = i1_sc[pl.ds(t0 + u, 1), :]
                r2 = i2_sc[pl.ds(t0 + u, 1), :]
                gg = g_sc[pl.ds(t0 + u, 1), :]
                a = jnp.where(sub == r1, gg, 0.0).astype(BF16)
                b = jnp.where(sub == r2, 1.0, 0.0).astype(BF16)
                row0 = base + u * TABLE_PITCH
                stage_sc[row0:row0 + N_KEYS, :] = lax.dot_general(
                    a, b, NT_DIMS, preferred_element_type=F32)
            for k1 in range(N_KEYS):
                blk = stage_sc[pl.ds(base + k1, TABLE_GROUP, stride=TABLE_PITCH), :]
                w_ref[pl.ds(t0, TABLE_GROUP), k1 * N_KEYS:(k1 + 1) * N_KEYS] = blk.astype(w_ref.dtype)
        return carry

    lax.fori_loop(0, tw // (TABLE_GROUP * TABLE_GROUPS_PER_STEP), groups, 0)


def peer_table(i1t, i2t, gt, *, tw=256):
    hk, s = i1t.shape
    tw = min(tw, s)
    assert tw % (TABLE_GROUP * TABLE_GROUPS_PER_STEP) == 0
    ispec = pl.BlockSpec((hk, tw), lambda i: (0, i))
    return pl.pallas_call(
        functools.partial(_peer_table_kernel, tw=tw),
        grid=(s // tw,),
        in_specs=[ispec, ispec, ispec],
        out_specs=pl.BlockSpec((tw, N_KEYS * N_KEYS), lambda i: (i, 0)),
        out_shape=jax.ShapeDtypeStruct((s, N_KEYS * N_KEYS), BF16),
        scratch_shapes=[pltpu.VMEM((tw, hk), jnp.int32), pltpu.VMEM((tw, hk), jnp.int32),
                        pltpu.VMEM((tw, hk), F32),
                        pltpu.VMEM((TABLE_GROUPS_PER_STEP * TABLE_GROUP * TABLE_PITCH, N_KEYS), F32)],
        compiler_params=_cparams("parallel"),
        name="peer_table",
    )(i1t, i2t, gt)


def _peer_experts_kernel(n_ref, w_ref, u_ref, v_ref, o_ref):
    j = pl.program_id(1)

    @pl.when(j == 0)
    def _():
        o_ref[...] = jnp.zeros_like(o_ref)

    act = lax.dot_general(n_ref[...], u_ref[...].astype(BF16), NT_DIMS, preferred_element_type=F32)
    m = (jax.nn.gelu(act) * w_ref[...].astype(F32)).astype(BF16)
    o_ref[...] += jnp.dot(m, v_ref[...].astype(BF16), preferred_element_type=F32)


def peer_experts(n2, w_flat, u, v, *, tm=1024, te=1024):
    s, d = n2.shape
    e = u.shape[0]
    tm = min(tm, s)
    once = pl.Buffered(1)
    return pl.pallas_call(
        _peer_experts_kernel,
        grid=(s // tm, e // te),
        in_specs=[
            pl.BlockSpec((tm, d), lambda i, j: (i, 0), pipeline_mode=once),
            pl.BlockSpec((tm, te), lambda i, j: (i, j)),
            pl.BlockSpec((te, d), lambda i, j: (j, 0)),
            pl.BlockSpec((te, d), lambda i, j: (j, 0)),
        ],
        out_specs=pl.BlockSpec((tm, d), lambda i, j: (i, 0), pipeline_mode=once),
        out_shape=jax.ShapeDtypeStruct((s, d), F32),
        compiler_params=_cparams("parallel", "arbitrary"),
        name="peer_experts",
    )(n2, w_flat, u, v)


def _ple_kernel(h_ref, f_ref, p_ref, gple_ref, wg_ref, wp_ref, gfin_ref, o_ref, *, final):
    h = h_ref[...] + f_ref[...]
    n3 = _rms(h, gple_ref[...]).astype(BF16)
    gate = jax.nn.sigmoid(jnp.dot(n3, wg_ref[...], preferred_element_type=F32))
    pp = jnp.dot(p_ref[...], wp_ref[...], preferred_element_type=F32)
    h = h + gate * pp
    if final:
        h = _rms(h, gfin_ref[...])
    o_ref[...] = h


def ple_block(h1, ffn_out, p, ple_norm, w_gate, w_proj, final_norm, *, final, tm=256):
    s, d = h1.shape
    pd = p.shape[1]
    tm = min(tm, s)
    full = lambda i: (0, 0)
    return pl.pallas_call(
        functools.partial(_ple_kernel, final=final),
        grid=(s // tm,),
        in_specs=[
            pl.BlockSpec((tm, d), lambda i: (i, 0)),
            pl.BlockSpec((tm, d), lambda i: (i, 0)),
            pl.BlockSpec((tm, pd), lambda i: (i, 0)),
            pl.BlockSpec((1, d), full),
            pl.BlockSpec((d, d), full),
            pl.BlockSpec((pd, d), full),
            pl.BlockSpec((1, d), full),
        ],
        out_specs=pl.BlockSpec((tm, d), lambda i: (i, 0)),
        out_shape=jax.ShapeDtypeStruct((s, d), F32),
        compiler_params=_cparams("parallel"),
        name="ple_block",
    )(h1, ffn_out, p, ple_norm.reshape(1, d), w_gate, w_proj, final_norm.reshape(1, d))


def _rot_cols(w):
    half = QK_ROPE_DIM // 2
    return jnp.concatenate([-w[..., half:], w[..., :half]], axis=-1)


def _layer(h, p, cos128, sin128, prm, *, final, final_norm):
    s, d = h.shape
    (attn_norm, w_in, b_gate, q_norm, w_uq, kv_norm, w_ukv, w_attn_o, conv_w, conv_b, w_rg_a, b_rg_a,
     w_rg_x, b_rg_x, lru_lambda, w_rnn_o, w_out, ffn_norm, w_peer_q, peer_subkeys, peer_u, peer_v,
     ple_norm, w_ple_gate, w_ple_proj) = prm

    o_q, o_kv, o_kr, o_x, o_y, o_g = 0, Q_LORA_RANK, Q_LORA_RANK + KV_LORA_RANK, \
        Q_LORA_RANK + KV_LORA_RANK + QK_ROPE_DIM, Q_LORA_RANK + KV_LORA_RANK + QK_ROPE_DIM + d, \
        Q_LORA_RANK + KV_LORA_RANK + QK_ROPE_DIM + 2 * d
    w_in_t = w_in.T
    w_kr = w_in_t[o_kr:o_x]
    w_kr_rot = _rot_cols(w_kr.T).T
    w_lat = jnp.concatenate([w_in_t[o_q:o_kr], w_kr, w_kr, w_kr_rot, w_kr_rot], axis=0).astype(BF16)
    w_x = w_in_t[o_x:o_y].astype(BF16)
    w_y = w_in_t[o_y:o_g].astype(BF16)
    w_g = w_in_t[o_g:].astype(BF16)
    uq = w_uq.reshape(Q_LORA_RANK, N_HEADS, QK_HEAD_DIM)
    uq_pe = uq[:, :, QK_NOPE_DIM:]
    w_q = jnp.concatenate([uq[:, :, :QK_NOPE_DIM].reshape(Q_LORA_RANK, -1),
                           uq_pe.reshape(Q_LORA_RANK, -1),
                           _rot_cols(uq_pe).reshape(Q_LORA_RANK, -1)], axis=1).astype(BF16)
    n_lat = Q_LORA_RANK + KV_LORA_RANK + 2 * LANES
    n_qn = N_HEADS * QK_NOPE_DIM
    n_qp = N_HEADS * QK_ROPE_DIM
    scale = QK_HEAD_DIM ** -0.5 * math.log2(math.e)

    n1 = rmsnorm_bf16(h, attn_norm)

    def lat_body(acc, extra, outs):
        outs[0][...] = acc[:, :Q_LORA_RANK]
        outs[1][...] = acc[:, Q_LORA_RANK:Q_LORA_RANK + KV_LORA_RANK]
        outs[2][...] = acc[:, Q_LORA_RANK + KV_LORA_RANK:]

    tm_lat = 512
    c_q, c_kv, kr4 = _mm_call(
        "in_proj_latents", n1, w_lat, tm=tm_lat, tn=n_lat, body=lat_body, w_rows_are_outputs=True,
        out_shapes=[jax.ShapeDtypeStruct((s, Q_LORA_RANK), F32), jax.ShapeDtypeStruct((s, KV_LORA_RANK), F32),
                    jax.ShapeDtypeStruct((s, 2 * LANES), F32)],
        out_specs=[pl.BlockSpec((min(tm_lat, s), Q_LORA_RANK), lambda j, i: (i, 0)),
                   pl.BlockSpec((min(tm_lat, s), KV_LORA_RANK), lambda j, i: (i, 0)),
                   pl.BlockSpec((min(tm_lat, s), 2 * LANES), lambda j, i: (i, 0))])

    tm, tn = min(512, s), 1024

    def store_body(acc, extra, outs):
        outs[0][...] = acc.astype(outs[0].dtype)

    x_r = _mm_call("in_proj_x", n1, w_x, tm=tm, tn=tn, body=store_body, w_rows_are_outputs=True,
                   out_shapes=jax.ShapeDtypeStruct((s, d), F32), out_specs=_tile_spec(tm, tn))

    def gelu_body(acc, extra, outs):
        outs[0][...] = jax.nn.gelu(acc).astype(outs[0].dtype)

    gy = _mm_call("in_proj_y", n1, w_y, tm=tm, tn=tn, body=gelu_body, w_rows_are_outputs=True,
                  out_shapes=jax.ShapeDtypeStruct((s, d), BF16), out_specs=_tile_spec(tm, tn))

    def gate_body(acc, extra, outs):
        outs[0][...] = jax.nn.sigmoid(acc + extra[0][...]).astype(outs[0].dtype)

    gates = _mm_call("in_proj_gates", n1, w_g, tm=tm, tn=tn, body=gate_body, w_rows_are_outputs=True,
                     extras=(b_gate.reshape(1, 2 * d),), extra_specs=(_col_spec(tn),),
                     out_shapes=jax.ShapeDtypeStruct((s, 2 * d), BF16), out_specs=_tile_spec(tm, tn))

    def norm_prologue(a, extra):
        return _rms(a, extra[0][...]).astype(BF16)

    def q_body(acc, extra, outs):
        outs[0][...] = (acc[:, :n_qn] * scale).astype(BF16)
        cos_t = jnp.tile(extra[1][...], (1, n_qp // LANES))
        sin_t = jnp.tile(extra[2][...], (1, n_qp // LANES))
        pe = acc[:, n_qn:n_qn + n_qp] * cos_t + acc[:, n_qn + n_qp:] * sin_t
        outs[1][...] = (pe * scale).astype(BF16)

    tm_q = min(256, s)
    row128 = pl.BlockSpec((tm_q, LANES), lambda j, i: (i, 0))
    qn, qpe = _mm_call(
        "q_proj", c_q, w_q, tm=tm_q, tn=n_qn + 2 * n_qp, prologue=norm_prologue, body=q_body,
        extras=(q_norm.reshape(1, Q_LORA_RANK), cos128, sin128),
        extra_specs=(pl.BlockSpec((1, Q_LORA_RANK), lambda j, i: (0, 0)), row128, row128),
        out_shapes=[jax.ShapeDtypeStruct((s, n_qn), BF16), jax.ShapeDtypeStruct((s, n_qp), BF16)],
        out_specs=[pl.BlockSpec((tm_q, n_qn), lambda j, i: (i, 0)), pl.BlockSpec((tm_q, n_qp), lambda j, i: (i, 0))])

    def kv_body(acc, extra, outs):
        outs[0][...] = acc.astype(BF16)
        kr = extra[1][...]
        kpe2 = kr[:, :LANES] * extra[2][...] + kr[:, LANES:] * extra[3][...]
        lane = lax.broadcasted_iota(jnp.int32, kpe2.shape, 1)
        outs[1][...] = jnp.concatenate([jnp.where(lane < QK_ROPE_DIM, kpe2, 0.0),
                                        jnp.where(lane >= QK_ROPE_DIM, kpe2, 0.0)], axis=1).astype(BF16)

    n_kv = N_HEADS * (QK_NOPE_DIM + V_HEAD_DIM)
    kv, kpe = _mm_call(
        "kv_proj", c_kv, w_ukv.astype(BF16), tm=tm_q, tn=n_kv, prologue=norm_prologue, body=kv_body,
        extras=(kv_norm.reshape(1, KV_LORA_RANK), kr4, cos128, sin128),
        extra_specs=(pl.BlockSpec((1, KV_LORA_RANK), lambda j, i: (0, 0)),
                     pl.BlockSpec((tm_q, 2 * LANES), lambda j, i: (i, 0)), row128, row128),
        out_shapes=[jax.ShapeDtypeStruct((s, n_kv), BF16), jax.ShapeDtypeStruct((s, 2 * LANES), BF16)],
        out_specs=[pl.BlockSpec((tm_q, n_kv), lambda j, i: (i, 0)), pl.BlockSpec((tm_q, 2 * LANES), lambda j, i: (i, 0))])

    o_attn = mla_attention(qn, qpe, kv, kpe)
    y_rnn_in = rglru(x_r, gy, conv_w, conv_b, w_rg_a, b_rg_a, w_rg_x, b_rg_x, lru_lambda)

    def merge_kernel(o_ref, y_ref, wo_ref, wr_ref, ga_ref, gr_ref, m_ref):
        ya = jnp.dot(o_ref[...], wo_ref[...], preferred_element_type=F32)
        yr = jnp.dot(y_ref[...], wr_ref[...], preferred_element_type=F32)
        m_ref[...] = (ga_ref[...].astype(F32) * ya + gr_ref[...].astype(F32) * yr).astype(m_ref.dtype)

    a_spec = pl.BlockSpec((tm, d), lambda j, i: (i, 0))
    w_spec = pl.BlockSpec((d, tn), lambda j, i: (0, j))
    merged = pl.pallas_call(
        merge_kernel,
        grid=(d // tn, s // tm),
        in_specs=[a_spec, a_spec, w_spec, w_spec, _tile_spec(tm, tn), _tile_spec(tm, tn, d // tn)],
        out_specs=_tile_spec(tm, tn),
        out_shape=jax.ShapeDtypeStruct((s, d), BF16),
        compiler_params=_cparams("parallel", "parallel"),
        name="mixer_merge",
    )(o_attn, y_rnn_in, w_attn_o.astype(BF16), w_rnn_o.astype(BF16), gates, gates)

    def out_body(acc, extra, outs):
        h1 = extra[0][...] + acc
        outs[0][...] = h1
        outs[1][...] = _rms(h1, extra[1][...]).astype(BF16)

    tm_o = min(256, s)
    rowd = pl.BlockSpec((tm_o, d), lambda j, i: (i, 0))
    h1, n2 = _mm_call(
        "out_proj", merged, w_out.astype(BF16), tm=tm_o, tn=d, body=out_body,
        extras=(h, ffn_norm.reshape(1, d)), extra_specs=(rowd, pl.BlockSpec((1, d), lambda j, i: (0, 0))),
        out_shapes=[jax.ShapeDtypeStruct((s, d), F32), jax.ShapeDtypeStruct((s, d), BF16)],
        out_specs=[rowd, rowd])

    n_hc = 2 * PEER_HEADS

    def pq_body(acc, extra, outs):
        for c in range(n_hc):
            outs[0][c] = acc[:, c * PEER_HALF:(c + 1) * PEER_HALF]

    pq = _mm_call("peer_query", n2, w_peer_q.astype(BF16), tm=tm_o, tn=n_hc * PEER_HALF, body=pq_body,
                  out_shapes=jax.ShapeDtypeStruct((n_hc, s, PEER_HALF), F32),
                  out_specs=pl.BlockSpec((n_hc, tm_o, PEER_HALF), lambda j, i: (0, i, 0)))
    i1t, i2t, gt = peer_route(pq, peer_subkeys.reshape(n_hc, N_KEYS, PEER_HALF).astype(BF16))
    w_tab = peer_table(i1t, i2t, gt)
    ffn_out = peer_experts(n2, w_tab, peer_u, peer_v)

    return ple_block(h1, ffn_out, p.astype(BF16), ple_norm, w_ple_gate.astype(BF16), w_ple_proj.astype(BF16),
                     final_norm, final=final)


def kernel(x, p, attn_norm, w_in, b_gate, q_norm, w_uq, kv_norm, w_ukv, w_attn_o, conv_w, conv_b, w_rg_a, b_rg_a,
           w_rg_x, b_rg_x, lru_lambda, w_rnn_o, w_out, ffn_norm, w_peer_q, peer_subkeys, peer_u, peer_v, ple_norm,
           w_ple_gate, w_ple_proj, final_norm):
    bsz, s, d = x.shape
    assert bsz == 1, "one sequence per call"
    depth = w_in.shape[0]
    layer_params = (attn_norm, w_in, b_gate, q_norm, w_uq, kv_norm, w_ukv, w_attn_o, conv_w, conv_b, w_rg_a,
                    b_rg_a, w_rg_x, b_rg_x, lru_lambda, w_rnn_o, w_out, ffn_norm, w_peer_q, peer_subkeys, peer_u,
                    peer_v, ple_norm, w_ple_gate, w_ple_proj)

    half = QK_ROPE_DIM // 2
    inv_freq = 1.0 / (ROPE_THETA ** (jnp.arange(half, dtype=F32) / half))
    ang = jnp.arange(s, dtype=F32)[:, None] * inv_freq[None, :]
    cos128 = jnp.tile(jnp.cos(ang), (1, LANES // half))
    sin128 = jnp.tile(jnp.sin(ang), (1, LANES // half))

    h = x.reshape(s, d)
    for l in range(depth):
        h = _layer(h, p[l, 0], cos128, sin128, tuple(w[l] for w in layer_params),
                   final=(l == depth - 1), final_norm=final_norm)
    return h.reshape(bsz, s, d)
```

```python
import functools
import math

import jax
import jax.numpy as jnp
from jax import lax
from jax.experimental import pallas as pl
from jax.experimental.pallas import tpu as pltpu

RMS_EPS = 1e-6
N_HEADS = 16
QK_NOPE_DIM = 128
QK_ROPE_DIM = 64
V_HEAD_DIM = 128
QK_HEAD_DIM = QK_NOPE_DIM + QK_ROPE_DIM
Q_LORA_RANK = 768
KV_LORA_RANK = 512
ROPE_THETA = 10000.0
RNN_BLOCKS = 16
RNN_BLOCK_DIM = 128
CONV_WIDTH = 4
LRU_C = 8.0
PEER_HEADS = 8
N_KEYS = 128
PEER_HALF = 128
PEER_TOPK = 16

LANES = 128
SUBLANES = 8
VMEM_LIMIT_BYTES = 56 * 1024 * 1024

F32 = jnp.float32
BF16 = jnp.bfloat16
NT_DIMS = (((1,), (1,)), ((), ()))


def _cparams(*sem):
    return pltpu.CompilerParams(dimension_semantics=sem, vmem_limit_bytes=VMEM_LIMIT_BYTES)


def _rms(x, g):
    ms = jnp.mean(x * x, axis=-1, keepdims=True)
    return (x * lax.rsqrt(ms + RMS_EPS)) * g


def _rmsnorm_kernel(x_ref, g_ref, o_ref):
    o_ref[...] = _rms(x_ref[...], g_ref[...]).astype(o_ref.dtype)


def rmsnorm_bf16(x, g, *, tr=512):
    m, d = x.shape
    tr = min(tr, m)
    return pl.pallas_call(
        _rmsnorm_kernel,
        grid=(m // tr,),
        in_specs=[pl.BlockSpec((tr, d), lambda i: (i, 0)), pl.BlockSpec((1, d), lambda i: (0, 0))],
        out_specs=pl.BlockSpec((tr, d), lambda i: (i, 0)),
        out_shape=jax.ShapeDtypeStruct((m, d), BF16),
        compiler_params=_cparams("parallel"),
        name="rmsnorm",
    )(x, g.reshape(1, d))


def _mm_kernel(a_ref, w_ref, *refs, n_extra, prologue, body, w_rows_are_outputs):
    extra = refs[:n_extra]
    outs = refs[n_extra:]
    a = a_ref[...]
    if prologue is not None:
        a = prologue(a, extra)
    if w_rows_are_outputs:
        acc = lax.dot_general(a, w_ref[...], NT_DIMS, preferred_element_type=F32)
    else:
        acc = jnp.dot(a, w_ref[...], preferred_element_type=F32)
    body(acc, extra, outs)


def _mm_call(name, a, w, *, tm, tn, extras=(), extra_specs=(), out_shapes, out_specs, body, prologue=None,
             w_rows_are_outputs=False):
    m, k = a.shape
    n = w.shape[0] if w_rows_are_outputs else w.shape[1]
    tm = min(tm, m)
    assert m % tm == 0 and n % tn == 0
    kern = functools.partial(_mm_kernel, n_extra=len(extras), prologue=prologue, body=body,
                             w_rows_are_outputs=w_rows_are_outputs)
    w_spec = pl.BlockSpec((tn, k), lambda j, i: (j, 0)) if w_rows_are_outputs else \
        pl.BlockSpec((k, tn), lambda j, i: (0, j))
    return pl.pallas_call(
        kern,
        grid=(n // tn, m // tm),
        in_specs=[pl.BlockSpec((tm, k), lambda j, i: (i, 0)), w_spec]
        + list(extra_specs),
        out_specs=out_specs,
        out_shape=out_shapes,
        compiler_params=_cparams("parallel", "parallel"),
        name=name,
    )(a, w, *extras)


def _tile_spec(tm, tn, col_off_blocks=0):
    return pl.BlockSpec((tm, tn), lambda j, i: (i, j + col_off_blocks))


def _col_spec(tn):
    return pl.BlockSpec((1, tn), lambda j, i: (0, j))


ATTN_ROW_CHUNK = 32


def _attn_kernel(qn_ref, qpe_ref, kv_ref, kpe_ref, o_ref, s0_sc, s1_sc, p0_sc, p1_sc, a0_sc, a1_sc,
                 m_sc, l_sc, acc_sc, *, tq):
    i = pl.program_id(1)
    rc = ATTN_ROW_CHUNK
    qpe = qpe_ref[...]
    qs = [jnp.concatenate([qn_ref[:, hh * LANES:(hh + 1) * LANES], qpe], axis=1) for hh in range(2)]
    m_sc[...] = jnp.full(m_sc.shape, -jnp.inf, F32)
    l_sc[...] = jnp.zeros(l_sc.shape, F32)
    acc_sc[...] = jnp.zeros(acc_sc.shape, F32)
    chunks = [slice(c * rc, (c + 1) * rc) for c in range(tq // rc)]
    even = (s0_sc, p0_sc, a0_sc)
    odd = (s1_sc, p1_sc, a1_sc)

    def scores(kb, bufs):
        s_sc = bufs[0]
        start = pl.multiple_of(kb * tq, tq)
        for hh in range(2):
            k = jnp.concatenate(
                [kv_ref[pl.ds(start, tq), hh * 256:hh * 256 + LANES],
                 kpe_ref[pl.ds(start, tq), hh * LANES:(hh + 1) * LANES]], axis=1)
            s_sc[hh] = lax.dot_general(qs[hh], k, NT_DIMS, preferred_element_type=F32)

    def mask_diagonal(bufs):
        s_sc = bufs[0]
        for hh in range(2):
            for rows in chunks:
                row = lax.broadcasted_iota(jnp.int32, (rc, tq), 0) + rows.start
                col = lax.broadcasted_iota(jnp.int32, (rc, tq), 1)
                s_sc[hh, rows, :] = jnp.where(col <= row, s_sc[hh, rows, :], -jnp.inf)

    def accumulate(kb, bufs):
        s_sc, p_sc, a_sc = bufs
        start = pl.multiple_of(kb * tq, tq)
        for hh in range(2):
            for rows in chunks:
                m_old = m_sc[hh, rows, :]
                m_new = jnp.maximum(m_old, jnp.max(s_sc[hh, rows, :], axis=-1, keepdims=True))
                a_sc[hh, rows, :] = jnp.exp2(m_old - m_new)
                m_sc[hh, rows, :] = m_new
            for rows in chunks:
                m_new = jnp.concatenate([m_sc[hh, rows, :]] * (tq // LANES), axis=1)
                p = jnp.exp2(s_sc[hh, rows, :] - m_new)
                l_sc[hh, rows, :] = a_sc[hh, rows, :] * l_sc[hh, rows, :] + jnp.sum(p, axis=-1, keepdims=True)
                p_sc[hh, rows, :] = p.astype(BF16)
        for hh in range(2):
            v = kv_ref[pl.ds(start, tq), hh * 256 + LANES:(hh + 1) * 256]
            acc_sc[hh] = a_sc[hh] * acc_sc[hh] + jnp.dot(p_sc[hh], v, preferred_element_type=F32)

    scores(0, even)

    def two_blocks(kk, carry):
        kb = 2 * kk + 1
        scores(kb, odd)
        accumulate(kb - 1, even)
        scores(kb + 1, even)
        accumulate(kb, odd)
        return carry

    lax.fori_loop(0, i // 2, two_blocks, 0)

    @pl.when(i % 2 == 1)
    def _():
        scores(i, odd)
        accumulate(i - 1, even)
        mask_diagonal(odd)
        accumulate(i, odd)

    @pl.when(i % 2 == 0)
    def _():
        mask_diagonal(even)
        accumulate(i, even)

    for hh in range(2):
        o_ref[:, hh * LANES:(hh + 1) * LANES] = (acc_sc[hh] / l_sc[hh]).astype(o_ref.dtype)


def mla_attention(qn, qpe, kv, kpe, *, tq=512):
    s = qn.shape[0]
    tq = min(tq, s)
    n_pairs = N_HEADS // 2
    return pl.pallas_call(
        functools.partial(_attn_kernel, tq=tq),
        grid=(n_pairs, s // tq),
        in_specs=[
            pl.BlockSpec((tq, 2 * LANES), lambda j, i: (i, j)),
            pl.BlockSpec((tq, LANES), lambda j, i: (i, j)),
            pl.BlockSpec((s, 4 * LANES), lambda j, i: (0, j)),
            pl.BlockSpec((s, 2 * LANES), lambda j, i: (0, 0)),
        ],
        out_specs=pl.BlockSpec((tq, 2 * LANES), lambda j, i: (i, j)),
        out_shape=jax.ShapeDtypeStruct((s, N_HEADS * V_HEAD_DIM), BF16),
        scratch_shapes=[pltpu.VMEM((2, tq, tq), F32), pltpu.VMEM((2, tq, tq), F32),
                        pltpu.VMEM((2, tq, tq), BF16), pltpu.VMEM((2, tq, tq), BF16),
                        pltpu.VMEM((2, tq, LANES), F32), pltpu.VMEM((2, tq, LANES), F32),
                        pltpu.VMEM((2, tq, LANES), F32), pltpu.VMEM((2, tq, LANES), F32),
                        pltpu.VMEM((2, tq, V_HEAD_DIM), F32)],
        compiler_params=_cparams("parallel", "parallel"),
        name="mla_attention",
    )(qn, qpe, kv, kpe)


def _softplus(z):
    return jnp.maximum(z, 0.0) + jnp.log1p(jnp.exp(-jnp.abs(z)))


RGLRU_BLOCKS_PER_STEP = 4


def _rglru_kernel(x_ref, gy_ref, cw_ref, cb_ref, wa_ref, ba_ref, wx_ref, bx_ref, lam_ref, y_ref,
                  h_sc, edge_sc, a_sc, b_sc, *, ts):
    tb = pl.program_id(1)

    @pl.when(tb == 0)
    def _():
        h_sc[...] = jnp.zeros_like(h_sc)
        edge_sc[0:SUBLANES, :] = jnp.zeros((SUBLANES, edge_sc.shape[1]), F32)
        a_sc[0:SUBLANES, :] = jnp.ones((SUBLANES, a_sc.shape[1]), F32)
        b_sc[0:SUBLANES, :] = jnp.zeros((SUBLANES, b_sc.shape[1]), F32)

    row8 = lax.broadcasted_iota(jnp.int32, (SUBLANES, LANES), 0)
    for c in range(RGLRU_BLOCKS_PER_STEP):
        ln = slice(c * LANES, (c + 1) * LANES)
        x = x_ref[:, ln]
        cw = cw_ref[:, ln]
        edge_sc[SUBLANES:2 * SUBLANES, ln] = x[0:SUBLANES, :]
        xc = x * cw[CONV_WIDTH - 1:CONV_WIDTH, :] + cb_ref[:, ln]
        for d in range(1, CONV_WIDTH):
            xs = jnp.concatenate([edge_sc[SUBLANES - d:2 * SUBLANES - d, ln], x_ref[SUBLANES - d:ts - d, ln]], axis=0)
            xc = xc + xs * cw[CONV_WIDTH - 1 - d:CONV_WIDTH - d, :]
        edge_sc[0:SUBLANES, ln] = x[ts - SUBLANES:, :]

        xcb = xc.astype(BF16)
        r = jax.nn.sigmoid(jnp.dot(xcb, wa_ref[c], preferred_element_type=F32) + ba_ref[c])
        ig = jax.nn.sigmoid(jnp.dot(xcb, wx_ref[c], preferred_element_type=F32) + bx_ref[c])
        log_a = (-LRU_C) * r * _softplus(-lam_ref[:, ln])
        a = jnp.exp(log_a)
        th = jnp.tanh(log_a)
        m2 = -2.0 * th / (1.0 - th)
        mult = jnp.where(m2 > 0.0, m2 * lax.rsqrt(m2), 0.0)
        gated = ig * xc
        b = mult * gated
        first = jnp.where((row8 == 0) & (tb == 0), gated[0:SUBLANES, :], b[0:SUBLANES, :])
        b = jnp.concatenate([first, b[SUBLANES:, :]], axis=0)

        d = 1
        while d < ts:
            if d < SUBLANES:
                a_sc[SUBLANES:, ln] = a
                b_sc[SUBLANES:, ln] = b
                a_sh = a_sc[SUBLANES - d:SUBLANES - d + ts, ln]
                b_sh = b_sc[SUBLANES - d:SUBLANES - d + ts, ln]
            else:
                a_sh = jnp.concatenate([jnp.ones((d, LANES), F32), a[:ts - d]], axis=0)
                b_sh = jnp.concatenate([jnp.zeros((d, LANES), F32), b[:ts - d]], axis=0)
            b = a * b_sh + b
            a = a * a_sh
            d *= 2
        h = a * h_sc[0:1, ln] + b
        h_sc[:, ln] = jnp.broadcast_to(h[ts - 1:ts, :], (SUBLANES, LANES))
        y_ref[:, ln] = (h * gy_ref[:, ln].astype(F32)).astype(y_ref.dtype)


def rglru(x_r, gy, conv_w, conv_b, w_rg_a, b_rg_a, w_rg_x, b_rg_x, lru_lambda, *, ts=512):
    s, c = x_r.shape
    ts = min(ts, s)
    nblk = RGLRU_BLOCKS_PER_STEP
    wide = nblk * LANES
    n_gate_blocks = c // RNN_BLOCK_DIM
    col = lambda cb, tb: (0, cb)
    blk = lambda cb, tb: (cb, 0, 0)
    return pl.pallas_call(
        functools.partial(_rglru_kernel, ts=ts),
        grid=(c // wide, s // ts),
        in_specs=[
            pl.BlockSpec((ts, wide), lambda cb, tb: (tb, cb)),
            pl.BlockSpec((ts, wide), lambda cb, tb: (tb, cb)),
            pl.BlockSpec((CONV_WIDTH, wide), col),
            pl.BlockSpec((1, wide), col),
            pl.BlockSpec((nblk, RNN_BLOCK_DIM, RNN_BLOCK_DIM), blk),
            pl.BlockSpec((nblk, 1, RNN_BLOCK_DIM), blk),
            pl.BlockSpec((nblk, RNN_BLOCK_DIM, RNN_BLOCK_DIM), blk),
            pl.BlockSpec((nblk, 1, RNN_BLOCK_DIM), blk),
            pl.BlockSpec((1, wide), col),
        ],
        out_specs=pl.BlockSpec((ts, wide), lambda cb, tb: (tb, cb)),
        out_shape=jax.ShapeDtypeStruct((s, c), BF16),
        scratch_shapes=[pltpu.VMEM((SUBLANES, wide), F32), pltpu.VMEM((2 * SUBLANES, wide), F32),
                        pltpu.VMEM((SUBLANES + ts, wide), F32), pltpu.VMEM((SUBLANES + ts, wide), F32)],
        compiler_params=_cparams("parallel", "arbitrary"),
        name="rglru",
    )(x_r, gy, conv_w, conv_b.reshape(1, c), w_rg_a.astype(BF16), b_rg_a.reshape(n_gate_blocks, 1, RNN_BLOCK_DIM),
      w_rg_x.astype(BF16), b_rg_x.reshape(n_gate_blocks, 1, RNN_BLOCK_DIM), lru_lambda.reshape(1, c))


def _sorting_network(n):
    pairs = []

    def merge(lo, length, r):
        step = r * 2
        if step < length:
            merge(lo, length, step)
            merge(lo + r, length, step)
            pairs.extend((i, i + r) for i in range(lo + r, lo + length - r, step))
        else:
            pairs.append((lo, lo + r))

    def sort(lo, length):
        if length > 1:
            half = length // 2
            sort(lo, half)
            sort(lo + half, half)
            merge(lo, length, 1)

    sort(0, n)
    return pairs


def _topk_keys(s, k, vals_ref, idx_ref):
    n_slabs = s.shape[0] // SUBLANES
    assert n_slabs == k
    sub = lax.broadcasted_iota(jnp.int32, (SUBLANES, s.shape[1]), 0)
    vals = [s[v * SUBLANES:(v + 1) * SUBLANES, :] for v in range(n_slabs)]
    idxs = [sub + v * SUBLANES for v in range(n_slabs)]
    for i, j in _sorting_network(n_slabs):
        first = (vals[i] > vals[j]) | ((vals[i] == vals[j]) & (idxs[i] < idxs[j]))
        vals[i], vals[j] = jnp.where(first, vals[i], vals[j]), jnp.where(first, vals[j], vals[i])
        idxs[i], idxs[j] = jnp.where(first, idxs[i], idxs[j]), jnp.where(first, idxs[j], idxs[i])
    for r in range(k):
        m = jnp.max(vals[0], axis=0, keepdims=True)
        im = jnp.min(jnp.where(vals[0] == m, idxs[0], N_KEYS), axis=0, keepdims=True)
        vals_ref[r:r + 1, :] = m
        idx_ref[r:r + 1, :] = im
        taken = idxs[0] == im
        for v in range(k - 1 - r):
            vals[v] = jnp.where(taken, vals[v + 1], vals[v])
            idxs[v] = jnp.where(taken, idxs[v + 1], idxs[v])


def _topk_pairs(v0_sc, v1_sc, vals_ref, flat_ref):
    k = PEER_TOPK
    tt = v0_sc.shape[1]
    row8 = lax.broadcasted_iota(jnp.int32, (SUBLANES, tt), 0)
    v0_lo = v0_sc[0:SUBLANES, :]
    lists = [jnp.where(row8 < k // (b + 1), v0_lo + v1_sc[b:b + 1, :], -jnp.inf) for b in range(k)]
    single = v0_sc[SUBLANES:k, :] + v1_sc[0:1, :]
    pos = jnp.zeros((SUBLANES, tt), jnp.int32)
    flat_single = (row8 + SUBLANES) * k
    for r in range(k):
        flat_head = row8 * k + pos
        m = jnp.max(jnp.maximum(lists[0], single), axis=0, keepdims=True)
        best = jnp.min(jnp.minimum(jnp.where(lists[0] == m, flat_head, k * k),
                                   jnp.where(single == m, flat_single, k * k)), axis=0, keepdims=True)
        vals_ref[r:r + 1, :] = m
        flat_ref[r:r + 1, :] = best
        taken = flat_head == best
        single = jnp.where(flat_single == best, -jnp.inf, single)
        pos = pos + taken.astype(jnp.int32)
        for v in range(k - 1 - r):
            lists[v] = jnp.where(taken, lists[v + 1], lists[v])


def _peer_route_kernel(pq_ref, sk_ref, i1_ref, i2_ref, g_ref, v0_sc, i0_sc, v1_sc, j1_sc, bs_sc, bc_sc):
    k = PEER_TOPK
    for c, (v_sc, i_sc) in enumerate(((v0_sc, i0_sc), (v1_sc, j1_sc))):
        s_t = lax.dot_general(sk_ref[c], pq_ref[c].astype(BF16), NT_DIMS, preferred_element_type=F32)
        _topk_keys(s_t, k, v_sc, i_sc)
    _topk_pairs(v0_sc, v1_sc, bs_sc, bc_sc)
    bs = bs_sc[...]
    bc = bc_sc[...]
    row = lax.broadcasted_iota(jnp.int32, bs.shape, 0)
    i0 = i0_sc[...]
    j1 = j1_sc[...]
    for r in range(k):
        ca = bc[r:r + 1, :] >> 4
        cb = bc[r:r + 1, :] & (k - 1)
        i1_ref[r:r + 1, :] = jnp.sum(jnp.where(row == ca, i0, 0), axis=0, keepdims=True)
        i2_ref[r:r + 1, :] = jnp.sum(jnp.where(row == cb, j1, 0), axis=0, keepdims=True)
    e = jnp.exp(bs - bs[0:1, :])
    g_ref[...] = e / jnp.sum(e, axis=0, keepdims=True)


def peer_route(pq, subkeys, *, tt=256):
    s = pq.shape[1]
    tt = min(tt, s)
    k = PEER_TOPK
    out = jax.ShapeDtypeStruct((PEER_HEADS * k, s), jnp.int32)
    ospec = pl.BlockSpec((k, tt), lambda i, h: (h, i))
    return pl.pallas_call(
        _peer_route_kernel,
        grid=(s // tt, PEER_HEADS),
        in_specs=[pl.BlockSpec((2, tt, PEER_HALF), lambda i, h: (h, i, 0)),
                  pl.BlockSpec((2, N_KEYS, PEER_HALF), lambda i, h: (h, 0, 0))],
        out_specs=[ospec, ospec, ospec],
        out_shape=[out, out, jax.ShapeDtypeStruct((PEER_HEADS * k, s), F32)],
        scratch_shapes=[pltpu.VMEM((k, tt), F32), pltpu.VMEM((k, tt), jnp.int32),
                        pltpu.VMEM((k, tt), F32), pltpu.VMEM((k, tt), jnp.int32),
                        pltpu.VMEM((k, tt), F32), pltpu.VMEM((k, tt), jnp.int32)],
        compiler_params=_cparams("parallel", "parallel"),
        name="peer_route",
    )(pq, subkeys)


TABLE_GROUP = 16
TABLE_PITCH = N_KEYS + SUBLANES
TABLE_GROUPS_PER_STEP = 4


def _peer_table_kernel(i1t_ref, i2t_ref, gt_ref, w_ref, i1_sc, i2_sc, g_sc, stage_sc, *, tw):
    i1_sc[...] = i1t_ref[...].T
    i2_sc[...] = i2t_ref[...].T
    g_sc[...] = gt_ref[...].T
    sub = lax.broadcasted_iota(jnp.int32, (N_KEYS, PEER_HEADS * PEER_TOPK), 0)

    def groups(gi, carry):
        for sg in range(TABLE_GROUPS_PER_STEP):
            t0 = pl.multiple_of((gi * TABLE_GROUPS_PER_STEP + sg) * TABLE_GROUP, TABLE_GROUP)
            base = sg * TABLE_GROUP * TABLE_PITCH
            for u in range(TABLE_GROUP):
                r1 = i1_sc[pl.ds(t0 + u, 1), :]
                r2 = i2_sc[pl.ds(t0 + u, 1), :]
                gg = g_sc[pl.ds(t0 + u, 1), :]
                a = jnp.where(sub == r1, gg, 0.0).astype(BF16)
                b = jnp.where(sub == r2, 1.0, 0.0).astype(BF16)
                row0 = base + u * TABLE_PITCH
                stage_sc[row0:row0 + N_KEYS, :] = lax.dot_general(
                    a, b, NT_DIMS, preferred_element_type=F32)
            for k1 in range(N_KEYS):
                blk = stage_sc[pl.ds(base + k1, TABLE_GROUP, stride=TABLE_PITCH), :]
                w_ref[pl.ds(t0, TABLE_GROUP), k1 * N_KEYS:(k1 + 1) * N_KEYS] = blk.astype(w_ref.dtype)
        return carry

    lax.fori_loop(0, tw // (TABLE_GROUP * TABLE_GROUPS_PER_STEP), groups, 0)


def peer_table(i1t, i2t, gt, *, tw=256):
    hk, s = i1t.shape
    tw = min(tw, s)
    assert tw % (TABLE_GROUP * TABLE_GROUPS_PER_STEP) == 0
    ispec = pl.BlockSpec((hk, tw), lambda i: (0, i))
    return pl.pallas_call(
        functools.partial(_peer_table_kernel, tw=tw),
        grid=(s // tw,),
        in_specs=[ispec, ispec, ispec],
        out_specs=pl.BlockSpec((tw, N_KEYS * N_KEYS), lambda i: (i, 0)),
        out_shape=jax.ShapeDtypeStruct((s, N_KEYS * N_KEYS), BF16),
        scratch_shapes=[pltpu.VMEM((tw, hk), jnp.int32), pltpu.VMEM((tw, hk), jnp.int32),
                        pltpu.VMEM((tw, hk), F32),
                        pltpu.VMEM((TABLE_GROUPS_PER_STEP * TABLE_GROUP * TABLE_PITCH, N_KEYS), F32)],
        compiler_params=_cparams("parallel"),
        name="peer_table",
    )(i1t, i2t, gt)


def _peer_experts_kernel(n_ref, w_ref, u_ref, v_ref, o_ref):
    j = pl.program_id(1)

    @pl.when(j == 0)
    def _():
        o_ref[...] = jnp.zeros_like(o_ref)

    act = lax.dot_general(n_ref[...], u_ref[...].astype(BF16), NT_DIMS, preferred_element_type=F32)
    m = (jax.nn.gelu(act) * w_ref[...].astype(F32)).astype(BF16)
    o_ref[...] += jnp.dot(m, v_ref[...].astype(BF16), preferred_element_type=F32)


def peer_experts(n2, w_flat, u, v, *, tm=1024, te=1024):
    s, d = n2.shape
    e = u.shape[0]
    tm = min(tm, s)
    once = pl.Buffered(1)
    return pl.pallas_call(
        _peer_experts_kernel,
        grid=(s // tm, e // te),
        in_specs=[
            pl.BlockSpec((tm, d), lambda i, j: (i, 0), pipeline_mode=once),
            pl.BlockSpec((tm, te), lambda i, j: (i, j)),
            pl.BlockSpec((te, d), lambda i, j: (j, 0)),
            pl.BlockSpec((te, d), lambda i, j: (j, 0)),
        ],
        out_specs=pl.BlockSpec((tm, d), lambda i, j: (i, 0), pipeline_mode=once),
        out_shape=jax.ShapeDtypeStruct((s, d), F32),
        compiler_params=_cparams("parallel", "arbitrary"),
        name="peer_experts",
    )(n2, w_flat, u, v)


def _ple_kernel(h_ref, f_ref, p_ref, gple_ref, wg_ref, wp_ref, gfin_ref, o_ref, *, final):
    h = h_ref[...] + f_ref[...]
    n3 = _rms(h, gple_ref[...]).astype(BF16)
    gate = jax.nn.sigmoid(jnp.dot(n3, wg_ref[...], preferred_element_type=F32))
    pp = jnp.dot(p_ref[...], wp_ref[...], preferred_element_type=F32)
    h = h + gate * pp
    if final:
        h = _rms(h, gfin_ref[...])
    o_ref[...] = h


def ple_block(h1, ffn_out, p, ple_norm, w_gate, w_proj, final_norm, *, final, tm=256):
    s, d = h1.shape
    pd = p.shape[1]
    tm = min(tm, s)
    full = lambda i: (0, 0)
    return pl.pallas_call(
        functools.partial(_ple_kernel, final=final),
        grid=(s // tm,),
        in_specs=[
            pl.BlockSpec((tm, d), lambda i: (i, 0)),
            pl.BlockSpec((tm, d), lambda i: (i, 0)),
            pl.BlockSpec((tm, pd), lambda i: (i, 0)),
            pl.BlockSpec((1, d), full),
            pl.BlockSpec((d, d), full),
            pl.BlockSpec((pd, d), full),
            pl.BlockSpec((1, d), full),
        ],
        out_specs=pl.BlockSpec((tm, d), lambda i: (i, 0)),
        out_shape=jax.ShapeDtypeStruct((s, d), F32),
        compiler_params=_cparams("parallel"),
        name="ple_block",
    )(h1, ffn_out, p, ple_norm.reshape(1, d), w_gate, w_proj, final_norm.reshape(1, d))


def _rot_cols(w):
    half = QK_ROPE_DIM // 2
    return jnp.concatenate([-w[..., half:], w[..., :half]], axis=-1)


def _layer(h, p, cos128, sin128, prm, *, final, final_norm):
    s, d = h.shape
    (attn_norm, w_in, b_gate, q_norm, w_uq, kv_norm, w_ukv, w_attn_o, conv_w, conv_b, w_rg_a, b_rg_a,
     w_rg_x, b_rg_x, lru_lambda, w_rnn_o, w_out, ffn_norm, w_peer_q, peer_subkeys, peer_u, peer_v,
     ple_norm, w_ple_gate, w_ple_proj) = prm

    o_q, o_kv, o_kr, o_x, o_y, o_g = 0, Q_LORA_RANK, Q_LORA_RANK + KV_LORA_RANK, \
        Q_LORA_RANK + KV_LORA_RANK + QK_ROPE_DIM, Q_LORA_RANK + KV_LORA_RANK + QK_ROPE_DIM + d, \
        Q_LORA_RANK + KV_LORA_RANK + QK_ROPE_DIM + 2 * d
    w_in_t = w_in.T
    w_kr = w_in_t[o_kr:o_x]
    w_kr_rot = _rot_cols(w_kr.T).T
    w_lat = jnp.concatenate([w_in_t[o_q:o_kr], w_kr, w_kr, w_kr_rot, w_kr_rot], axis=0).astype(BF16)
    w_x = w_in_t[o_x:o_y].astype(BF16)
    w_y = w_in_t[o_y:o_g].astype(BF16)
    w_g = w_in_t[o_g:].astype(BF16)
    uq = w_uq.reshape(Q_LORA_RANK, N_HEADS, QK_HEAD_DIM)
    uq_pe = uq[:, :, QK_NOPE_DIM:]
    w_q = jnp.concatenate([uq[:, :, :QK_NOPE_DIM].reshape(Q_LORA_RANK, -1),
                           uq_pe.reshape(Q_LORA_RANK, -1),
                           _rot_cols(uq_pe).reshape(Q_LORA_RANK, -1)], axis=1).astype(BF16)
    n_lat = Q_LORA_RANK + KV_LORA_RANK + 2 * LANES
    n_qn = N_HEADS * QK_NOPE_DIM
    n_qp = N_HEADS * QK_ROPE_DIM
    scale = QK_HEAD_DIM ** -0.5 * math.log2(math.e)

    n1 = rmsnorm_bf16(h, attn_norm)

    def lat_body(acc, extra, outs):
        outs[0][...] = acc[:, :Q_LORA_RANK]
        outs[1][...] = acc[:, Q_LORA_RANK:Q_LORA_RANK + KV_LORA_RANK]
        outs[2][...] = acc[:, Q_LORA_RANK + KV_LORA_RANK:]

    tm_lat = 512
    c_q, c_kv, kr4 = _mm_call(
        "in_proj_latents", n1, w_lat, tm=tm_lat, tn=n_lat, body=lat_body, w_rows_are_outputs=True,
        out_shapes=[jax.ShapeDtypeStruct((s, Q_LORA_RANK), F32), jax.ShapeDtypeStruct((s, KV_LORA_RANK), F32),
                    jax.ShapeDtypeStruct((s, 2 * LANES), F32)],
        out_specs=[pl.BlockSpec((min(tm_lat, s), Q_LORA_RANK), lambda j, i: (i, 0)),
                   pl.BlockSpec((min(tm_lat, s), KV_LORA_RANK), lambda j, i: (i, 0)),
                   pl.BlockSpec((min(tm_lat, s), 2 * LANES), lambda j, i: (i, 0))])

    tm, tn = min(1024, s), 1024

    def store_body(acc, extra, outs):
        outs[0][...] = acc.astype(outs[0].dtype)

    x_r = _mm_call("in_proj_x", n1, w_x, tm=tm, tn=tn, body=store_body, w_rows_are_outputs=True,
                   out_shapes=jax.ShapeDtypeStruct((s, d), F32), out_specs=_tile_spec(tm, tn))

    def gelu_body(acc, extra, outs):
        outs[0][...] = jax.nn.gelu(acc).astype(outs[0].dtype)

    gy = _mm_call("in_proj_y", n1, w_y, tm=tm, tn=tn, body=gelu_body, w_rows_are_outputs=True,
                  out_shapes=jax.ShapeDtypeStruct((s, d), BF16), out_specs=_tile_spec(tm, tn))

    def gate_body(acc, extra, outs):
        outs[0][...] = jax.nn.sigmoid(acc + extra[0][...]).astype(outs[0].dtype)

    gates = _mm_call("in_proj_gates", n1, w_g, tm=tm, tn=tn, body=gate_body, w_rows_are_outputs=True,
                     extras=(b_gate.reshape(1, 2 * d),), extra_specs=(_col_spec(tn),),
                     out_shapes=jax.ShapeDtypeStruct((s, 2 * d), BF16), out_specs=_tile_spec(tm, tn))

    def norm_prologue(a, extra):
        return _rms(a, extra[0][...]).astype(BF16)

    def q_body(acc, extra, outs):
        outs[0][...] = (acc[:, :n_qn] * scale).astype(BF16)
        cos_t = jnp.tile(extra[1][...], (1, n_qp // LANES))
        sin_t = jnp.tile(extra[2][...], (1, n_qp // LANES))
        pe = acc[:, n_qn:n_qn + n_qp] * cos_t + acc[:, n_qn + n_qp:] * sin_t
        outs[1][...] = (pe * scale).astype(BF16)

    tm_q = min(512, s)
    row128 = pl.BlockSpec((tm_q, LANES), lambda j, i: (i, 0))
    qn, qpe = _mm_call(
        "q_proj", c_q, w_q, tm=tm_q, tn=n_qn + 2 * n_qp, prologue=norm_prologue, body=q_body,
        extras=(q_norm.reshape(1, Q_LORA_RANK), cos128, sin128),
        extra_specs=(pl.BlockSpec((1, Q_LORA_RANK), lambda j, i: (0, 0)), row128, row128),
        out_shapes=[jax.ShapeDtypeStruct((s, n_qn), BF16), jax.ShapeDtypeStruct((s, n_qp), BF16)],
        out_specs=[pl.BlockSpec((tm_q, n_qn), lambda j, i: (i, 0)), pl.BlockSpec((tm_q, n_qp), lambda j, i: (i, 0))])

    def kv_body(acc, extra, outs):
        outs[0][...] = acc.astype(BF16)
        kr = extra[1][...]
        kpe2 = kr[:, :LANES] * extra[2][...] + kr[:, LANES:] * extra[3][...]
        lane = lax.broadcasted_iota(jnp.int32, kpe2.shape, 1)
        outs[1][...] = jnp.concatenate([jnp.where(lane < QK_ROPE_DIM, kpe2, 0.0),
                                        jnp.where(lane >= QK_ROPE_DIM, kpe2, 0.0)], axis=1).astype(BF16)

    n_kv = N_HEADS * (QK_NOPE_DIM + V_HEAD_DIM)
    kv, kpe = _mm_call(
        "kv_proj", c_kv, w_ukv.astype(BF16), tm=tm_q, tn=n_kv, prologue=norm_prologue, body=kv_body,
        extras=(kv_norm.reshape(1, KV_LORA_RANK), kr4, cos128, sin128),
        extra_specs=(pl.BlockSpec((1, KV_LORA_RANK), lambda j, i: (0, 0)),
                     pl.BlockSpec((tm_q, 2 * LANES), lambda j, i: (i, 0)), row128, row128),
        out_shapes=[jax.ShapeDtypeStruct((s, n_kv), BF16), jax.ShapeDtypeStruct((s, 2 * LANES), BF16)],
        out_specs=[pl.BlockSpec((tm_q, n_kv), lambda j, i: (i, 0)), pl.BlockSpec((tm_q, 2 * LANES), lambda j, i: (i, 0))])

    o_attn = mla_attention(qn, qpe, kv, kpe)
    y_rnn_in = rglru(x_r, gy, conv_w, conv_b, w_rg_a, b_rg_a, w_rg_x, b_rg_x, lru_lambda)

    def merge_kernel(o_ref, y_ref, wo_ref, wr_ref, ga_ref, gr_ref, m_ref):
        ya = jnp.dot(o_ref[...], wo_ref[...], preferred_element_type=F32)
        yr = jnp.dot(y_ref[...], wr_ref[...], preferred_element_type=F32)
        m_ref[...] = (ga_ref[...].astype(F32) * ya + gr_ref[...].astype(F32) * yr).astype(m_ref.dtype)

    a_spec = pl.BlockSpec((tm, d), lambda j, i: (i, 0))
    w_spec = pl.BlockSpec((d, tn), lambda j, i: (0, j))
    merged = pl.pallas_call(
        merge_kernel,
        grid=(d // tn, s // tm),
        in_specs=[a_spec, a_spec, w_spec, w_spec, _tile_spec(tm, tn), _tile_spec(tm, tn, d // tn)],
        out_specs=_tile_spec(tm, tn),
        out_shape=jax.ShapeDtypeStruct((s, d), BF16),
        compiler_params=_cparams("parallel", "parallel"),
        name="mixer_merge",
    )(o_attn, y_rnn_in, w_attn_o.astype(BF16), w_rnn_o.astype(BF16), gates, gates)

    def out_body(acc, extra, outs):
        h1 = extra[0][...] + acc
        outs[0][...] = h1
        outs[1][...] = _rms(h1, extra[1][...]).astype(BF16)

    tm_o = min(512, s)
    rowd = pl.BlockSpec((tm_o, d), lambda j, i: (i, 0))
    h1, n2 = _mm_call(
        "out_proj", merged, w_out.astype(BF16), tm=tm_o, tn=d, body=out_body,
        extras=(h, ffn_norm.reshape(1, d)), extra_specs=(rowd, pl.BlockSpec((1, d), lambda j, i: (0, 0))),
        out_shapes=[jax.ShapeDtypeStruct((s, d), F32), jax.ShapeDtypeStruct((s, d), BF16)],
        out_specs=[rowd, rowd])

    n_hc = 2 * PEER_HEADS

    def pq_body(acc, extra, outs):
        for c in range(n_hc):
            outs[0][c] = acc[:, c * PEER_HALF:(c + 1) * PEER_HALF]

    pq = _mm_call("peer_query", n2, w_peer_q.astype(BF16), tm=tm_o, tn=n_hc * PEER_HALF, body=pq_body,
                  out_shapes=jax.ShapeDtypeStruct((n_hc, s, PEER_HALF), F32),
                  out_specs=pl.BlockSpec((n_hc, tm_o, PEER_HALF), lambda j, i: (0, i, 0)))
    i1t, i2t, gt = peer_route(pq, peer_subkeys.reshape(n_hc, N_KEYS, PEER_HALF).astype(BF16))
    w_tab = peer_table(i1t, i2t, gt)
    ffn_out = peer_experts(n2, w_tab, peer_u, peer_v)

    return ple_block(h1, ffn_out, p.astype(BF16), ple_norm, w_ple_gate.astype(BF16), w_ple_proj.astype(BF16),
                     final_norm, final=final)


def kernel(x, p, attn_norm, w_in, b_gate, q_norm, w_uq, kv_norm, w_ukv, w_attn_o, conv_w, conv_b, w_rg_a, b_rg_a,
           w_rg_x, b_rg_x, lru_lambda, w_rnn_o, w_out, ffn_norm, w_peer_q, peer_subkeys, peer_u, peer_v, ple_norm,
           w_ple_gate, w_ple_proj, final_norm):
    bsz, s, d = x.shape
    assert bsz == 1, "one sequence per call"
    depth = w_in.shape[0]
    layer_params = (attn_norm, w_in, b_gate, q_norm, w_uq, kv_norm, w_ukv, w_attn_o, conv_w, conv_b, w_rg_a,
                    b_rg_a, w_rg_x, b_rg_x, lru_lambda, w_rnn_o, w_out, ffn_norm, w_peer_q, peer_subkeys, peer_u,
                    peer_v, ple_norm, w_ple_gate, w_ple_proj)

    half = QK_ROPE_DIM // 2
    inv_freq = 1.0 / (ROPE_THETA ** (jnp.arange(half, dtype=F32) / half))
    ang = jnp.arange(s, dtype=F32)[:, None] * inv_freq[None, :]
    cos128 = jnp.tile(jnp.cos(ang), (1, LANES // half))
    sin128 = jnp.tile(jnp.sin(ang), (1, LANES // half))

    h = x.reshape(s, d)
    for l in range(depth):
        h = _layer(h, p[l, 0], cos128, sin128, tuple(w[l] for w in layer_params),
                   final=(l == depth - 1), final_norm=final_norm)
    return h.reshape(bsz, s, d)
```

```python
import functools
import math

import jax
import jax.numpy as jnp
from jax import lax
from jax.experimental import pallas as pl
from jax.experimental.pallas import tpu as pltpu

RMS_EPS = 1e-6
N_HEADS = 16
QK_NOPE_DIM = 128
QK_ROPE_DIM = 64
V_HEAD_DIM = 128
QK_HEAD_DIM = QK_NOPE_DIM + QK_ROPE_DIM
Q_LORA_RANK = 768
KV_LORA_RANK = 512
ROPE_THETA = 10000.0
RNN_BLOCKS = 16
RNN_BLOCK_DIM = 128
CONV_WIDTH = 4
LRU_C = 8.0
PEER_HEADS = 8
N_KEYS = 128
PEER_HALF = 128
PEER_TOPK = 16

LANES = 128
SUBLANES = 8
VMEM_LIMIT_BYTES = 56 * 1024 * 1024

F32 = jnp.float32
BF16 = jnp.bfloat16
NT_DIMS = (((1,), (1,)), ((), ()))


def _cparams(*sem):
    return pltpu.CompilerParams(dimension_semantics=sem, vmem_limit_bytes=VMEM_LIMIT_BYTES)


def _rms(x, g):
    ms = jnp.mean(x * x, axis=-1, keepdims=True)
    return (x * lax.rsqrt(ms + RMS_EPS)) * g


def _rmsnorm_kernel(x_ref, g_ref, o_ref):
    o_ref[...] = _rms(x_ref[...], g_ref[...]).astype(o_ref.dtype)


def rmsnorm_bf16(x, g, *, tr=512):
    m, d = x.shape
    tr = min(tr, m)
    return pl.pallas_call(
        _rmsnorm_kernel,
        grid=(m // tr,),
        in_specs=[pl.BlockSpec((tr, d), lambda i: (i, 0)), pl.BlockSpec((1, d), lambda i: (0, 0))],
        out_specs=pl.BlockSpec((tr, d), lambda i: (i, 0)),
        out_shape=jax.ShapeDtypeStruct((m, d), BF16),
        compiler_params=_cparams("parallel"),
        name="rmsnorm",
    )(x, g.reshape(1, d))


def _mm_kernel(a_ref, w_ref, *refs, n_extra, prologue, body, w_rows_are_outputs):
    extra = refs[:n_extra]
    outs = refs[n_extra:]
    a = a_ref[...]
    if prologue is not None:
        a = prologue(a, extra)
    w = w_ref[...].astype(BF16)
    if w_rows_are_outputs:
        acc = lax.dot_general(a, w, NT_DIMS, preferred_element_type=F32)
    else:
        acc = jnp.dot(a, w, preferred_element_type=F32)
    body(acc, extra, outs)


def _mm_call(name, a, w, *, tm, tn, extras=(), extra_specs=(), out_shapes, out_specs, body, prologue=None,
             w_rows_are_outputs=False, w_row_range=None):
    m, k = a.shape
    tm = min(tm, m)
    if w_row_range is not None:
        assert w_rows_are_outputs
        row0, row1 = w_row_range
        n = row1 - row0
        assert row0 % SUBLANES == 0 and w.dtype == F32
        w_spec = pl.BlockSpec((pl.Element(tn), pl.Element(k)),
                              lambda j, i: (pl.multiple_of(row0 + j * tn, SUBLANES), 0))
    else:
        n = w.shape[0] if w_rows_are_outputs else w.shape[1]
        mode = dict(pipeline_mode=pl.Buffered(1)) if n == tn else {}
        w_spec = pl.BlockSpec((tn, k), lambda j, i: (j, 0), **mode) if w_rows_are_outputs else \
            pl.BlockSpec((k, tn), lambda j, i: (0, j), **mode)
    assert m % tm == 0 and n % tn == 0
    kern = functools.partial(_mm_kernel, n_extra=len(extras), prologue=prologue, body=body,
                             w_rows_are_outputs=w_rows_are_outputs)
    return pl.pallas_call(
        kern,
        grid=(n // tn, m // tm),
        in_specs=[pl.BlockSpec((tm, k), lambda j, i: (i, 0)), w_spec]
        + list(extra_specs),
        out_specs=out_specs,
        out_shape=out_shapes,
        compiler_params=_cparams("parallel", "parallel"),
        name=name,
    )(a, w, *extras)


def _tile_spec(tm, tn, col_off_blocks=0):
    return pl.BlockSpec((tm, tn), lambda j, i: (i, j + col_off_blocks))


def _col_spec(tn):
    return pl.BlockSpec((1, tn), lambda j, i: (0, j))


ATTN_ROW_CHUNK = 32


def _attn_kernel(qn_ref, qpe_ref, kv_ref, kpe_ref, o_ref, s0_sc, s1_sc, p0_sc, p1_sc, a0_sc, a1_sc,
                 m_sc, l_sc, acc_sc, *, tq):
    i = pl.program_id(1)
    rc = ATTN_ROW_CHUNK
    qpe = qpe_ref[...]
    qs = [jnp.concatenate([qn_ref[:, hh * LANES:(hh + 1) * LANES], qpe], axis=1) for hh in range(2)]
    m_sc[...] = jnp.full(m_sc.shape, -jnp.inf, F32)
    l_sc[...] = jnp.zeros(l_sc.shape, F32)
    acc_sc[...] = jnp.zeros(acc_sc.shape, F32)
    chunks = [slice(c * rc, (c + 1) * rc) for c in range(tq // rc)]
    even = (s0_sc, p0_sc, a0_sc)
    odd = (s1_sc, p1_sc, a1_sc)

    def scores(kb, bufs):
        s_sc = bufs[0]
        start = pl.multiple_of(kb * tq, tq)
        for hh in range(2):
            k = jnp.concatenate(
                [kv_ref[pl.ds(start, tq), hh * 256:hh * 256 + LANES],
                 kpe_ref[pl.ds(start, tq), hh * LANES:(hh + 1) * LANES]], axis=1)
            s_sc[hh] = lax.dot_general(qs[hh], k, NT_DIMS, preferred_element_type=F32)

    def mask_diagonal(bufs):
        s_sc = bufs[0]
        for hh in range(2):
            for rows in chunks:
                row = lax.broadcasted_iota(jnp.int32, (rc, tq), 0) + rows.start
                col = lax.broadcasted_iota(jnp.int32, (rc, tq), 1)
                s_sc[hh, rows, :] = jnp.where(col <= row, s_sc[hh, rows, :], -jnp.inf)

    def accumulate(kb, bufs):
        s_sc, p_sc, a_sc = bufs
        start = pl.multiple_of(kb * tq, tq)
        for hh in range(2):
            for rows in chunks:
                m_old = m_sc[hh, rows, :]
                m_new = jnp.maximum(m_old, jnp.max(s_sc[hh, rows, :], axis=-1, keepdims=True))
                a_sc[hh, rows, :] = jnp.exp2(m_old - m_new)
                m_sc[hh, rows, :] = m_new
            for rows in chunks:
                m_new = jnp.concatenate([m_sc[hh, rows, :]] * (tq // LANES), axis=1)
                p = jnp.exp2(s_sc[hh, rows, :] - m_new)
                l_sc[hh, rows, :] = a_sc[hh, rows, :] * l_sc[hh, rows, :] + jnp.sum(p, axis=-1, keepdims=True)
                p_sc[hh, rows, :] = p.astype(BF16)
        for hh in range(2):
            v = kv_ref[pl.ds(start, tq), hh * 256 + LANES:(hh + 1) * 256]
            acc_sc[hh] = a_sc[hh] * acc_sc[hh] + jnp.dot(p_sc[hh], v, preferred_element_type=F32)

    scores(0, even)

    def two_blocks(kk, carry):
        kb = 2 * kk + 1
        scores(kb, odd)
        accumulate(kb - 1, even)
        scores(kb + 1, even)
        accumulate(kb, odd)
        return carry

    lax.fori_loop(0, i // 2, two_blocks, 0)

    @pl.when(i % 2 == 1)
    def _():
        scores(i, odd)
        accumulate(i - 1, even)
        mask_diagonal(odd)
        accumulate(i, odd)

    @pl.when(i % 2 == 0)
    def _():
        mask_diagonal(even)
        accumulate(i, even)

    for hh in range(2):
        o_ref[:, hh * LANES:(hh + 1) * LANES] = (acc_sc[hh] / l_sc[hh]).astype(o_ref.dtype)


def mla_attention(qn, qpe, kv, kpe, *, tq=512):
    s = qn.shape[0]
    tq = min(tq, s)
    n_pairs = N_HEADS // 2
    return pl.pallas_call(
        functools.partial(_attn_kernel, tq=tq),
        grid=(n_pairs, s // tq),
        in_specs=[
            pl.BlockSpec((tq, 2 * LANES), lambda j, i: (i, j)),
            pl.BlockSpec((tq, LANES), lambda j, i: (i, j)),
            pl.BlockSpec((s, 4 * LANES), lambda j, i: (0, j)),
            pl.BlockSpec((s, 2 * LANES), lambda j, i: (0, 0)),
        ],
        out_specs=pl.BlockSpec((tq, 2 * LANES), lambda j, i: (i, j)),
        out_shape=jax.ShapeDtypeStruct((s, N_HEADS * V_HEAD_DIM), BF16),
        scratch_shapes=[pltpu.VMEM((2, tq, tq), F32), pltpu.VMEM((2, tq, tq), F32),
                        pltpu.VMEM((2, tq, tq), BF16), pltpu.VMEM((2, tq, tq), BF16),
                        pltpu.VMEM((2, tq, LANES), F32), pltpu.VMEM((2, tq, LANES), F32),
                        pltpu.VMEM((2, tq, LANES), F32), pltpu.VMEM((2, tq, LANES), F32),
                        pltpu.VMEM((2, tq, V_HEAD_DIM), F32)],
        compiler_params=_cparams("parallel", "parallel"),
        name="mla_attention",
    )(qn, qpe, kv, kpe)


def _softplus(z):
    return jnp.maximum(z, 0.0) + jnp.log1p(jnp.exp(-jnp.abs(z)))


RGLRU_BLOCKS_PER_STEP = 4


def _rglru_kernel(x_ref, gy_ref, cw_ref, cb_ref, wa_ref, ba_ref, wx_ref, bx_ref, lam_ref, y_ref,
                  h_sc, edge_sc, a_sc, b_sc, *, ts):
    tb = pl.program_id(1)

    @pl.when(tb == 0)
    def _():
        h_sc[...] = jnp.zeros_like(h_sc)
        edge_sc[0:SUBLANES, :] = jnp.zeros((SUBLANES, edge_sc.shape[1]), F32)
        a_sc[0:SUBLANES, :] = jnp.ones((SUBLANES, a_sc.shape[1]), F32)
        b_sc[0:SUBLANES, :] = jnp.zeros((SUBLANES, b_sc.shape[1]), F32)

    row8 = lax.broadcasted_iota(jnp.int32, (SUBLANES, LANES), 0)
    for c in range(RGLRU_BLOCKS_PER_STEP):
        ln = slice(c * LANES, (c + 1) * LANES)
        x = x_ref[:, ln]
        cw = cw_ref[:, ln]
        edge_sc[SUBLANES:2 * SUBLANES, ln] = x[0:SUBLANES, :]
        xc = x * cw[CONV_WIDTH - 1:CONV_WIDTH, :] + cb_ref[:, ln]
        for d in range(1, CONV_WIDTH):
            xs = jnp.concatenate([edge_sc[SUBLANES - d:2 * SUBLANES - d, ln], x_ref[SUBLANES - d:ts - d, ln]], axis=0)
            xc = xc + xs * cw[CONV_WIDTH - 1 - d:CONV_WIDTH - d, :]
        edge_sc[0:SUBLANES, ln] = x[ts - SUBLANES:, :]

        xcb = xc.astype(BF16)
        r = jax.nn.sigmoid(jnp.dot(xcb, wa_ref[c], preferred_element_type=F32) + ba_ref[c])
        ig = jax.nn.sigmoid(jnp.dot(xcb, wx_ref[c], preferred_element_type=F32) + bx_ref[c])
        log_a = (-LRU_C) * r * _softplus(-lam_ref[:, ln])
        a = jnp.exp(log_a)
        th = jnp.tanh(log_a)
        m2 = -2.0 * th / (1.0 - th)
        mult = jnp.where(m2 > 0.0, m2 * lax.rsqrt(m2), 0.0)
        gated = ig * xc
        b = mult * gated
        first = jnp.where((row8 == 0) & (tb == 0), gated[0:SUBLANES, :], b[0:SUBLANES, :])
        b = jnp.concatenate([first, b[SUBLANES:, :]], axis=0)

        d = 1
        while d < ts:
            if d < SUBLANES:
                a_sc[SUBLANES:, ln] = a
                b_sc[SUBLANES:, ln] = b
                a_sh = a_sc[SUBLANES - d:SUBLANES - d + ts, ln]
                b_sh = b_sc[SUBLANES - d:SUBLANES - d + ts, ln]
            else:
                a_sh = jnp.concatenate([jnp.ones((d, LANES), F32), a[:ts - d]], axis=0)
                b_sh = jnp.concatenate([jnp.zeros((d, LANES), F32), b[:ts - d]], axis=0)
            b = a * b_sh + b
            a = a * a_sh
            d *= 2
        h = a * h_sc[0:1, ln] + b
        h_sc[:, ln] = jnp.broadcast_to(h[ts - 1:ts, :], (SUBLANES, LANES))
        y_ref[:, ln] = (h * gy_ref[:, ln].astype(F32)).astype(y_ref.dtype)


def rglru(x_r, gy, conv_w, conv_b, w_rg_a, b_rg_a, w_rg_x, b_rg_x, lru_lambda, *, ts=512):
    s, c = x_r.shape
    ts = min(ts, s)
    nblk = RGLRU_BLOCKS_PER_STEP
    wide = nblk * LANES
    n_gate_blocks = c // RNN_BLOCK_DIM
    col = lambda cb, tb: (0, cb)
    blk = lambda cb, tb: (cb, 0, 0)
    return pl.pallas_call(
        functools.partial(_rglru_kernel, ts=ts),
        grid=(c // wide, s // ts),
        in_specs=[
            pl.BlockSpec((ts, wide), lambda cb, tb: (tb, cb)),
            pl.BlockSpec((ts, wide), lambda cb, tb: (tb, cb)),
            pl.BlockSpec((CONV_WIDTH, wide), col),
            pl.BlockSpec((1, wide), col),
            pl.BlockSpec((nblk, RNN_BLOCK_DIM, RNN_BLOCK_DIM), blk),
            pl.BlockSpec((nblk, 1, RNN_BLOCK_DIM), blk),
            pl.BlockSpec((nblk, RNN_BLOCK_DIM, RNN_BLOCK_DIM), blk),
            pl.BlockSpec((nblk, 1, RNN_BLOCK_DIM), blk),
            pl.BlockSpec((1, wide), col),
        ],
        out_specs=pl.BlockSpec((ts, wide), lambda cb, tb: (tb, cb)),
        out_shape=jax.ShapeDtypeStruct((s, c), BF16),
        scratch_shapes=[pltpu.VMEM((SUBLANES, wide), F32), pltpu.VMEM((2 * SUBLANES, wide), F32),
                        pltpu.VMEM((SUBLANES + ts, wide), F32), pltpu.VMEM((SUBLANES + ts, wide), F32)],
        compiler_params=_cparams("parallel", "arbitrary"),
        name="rglru",
    )(x_r, gy, conv_w, conv_b.reshape(1, c), w_rg_a.astype(BF16), b_rg_a.reshape(n_gate_blocks, 1, RNN_BLOCK_DIM),
      w_rg_x.astype(BF16), b_rg_x.reshape(n_gate_blocks, 1, RNN_BLOCK_DIM), lru_lambda.reshape(1, c))


def _sorting_network(n):
    pairs = []

    def merge(lo, length, r):
        step = r * 2
        if step < length:
            merge(lo, length, step)
            merge(lo + r, length, step)
            pairs.extend((i, i + r) for i in range(lo + r, lo + length - r, step))
        else:
            pairs.append((lo, lo + r))

    def sort(lo, length):
        if length > 1:
            half = length // 2
            sort(lo, half)
            sort(lo + half, half)
            merge(lo, length, 1)

    sort(0, n)
    return pairs


def _topk_keys(s, k, vals_ref, idx_ref):
    n_slabs = s.shape[0] // SUBLANES
    assert n_slabs == k
    sub = lax.broadcasted_iota(jnp.int32, (SUBLANES, s.shape[1]), 0)
    vals = [s[v * SUBLANES:(v + 1) * SUBLANES, :] for v in range(n_slabs)]
    idxs = [sub + v * SUBLANES for v in range(n_slabs)]
    for i, j in _sorting_network(n_slabs):
        first = (vals[i] > vals[j]) | ((vals[i] == vals[j]) & (idxs[i] < idxs[j]))
        vals[i], vals[j] = jnp.where(first, vals[i], vals[j]), jnp.where(first, vals[j], vals[i])
        idxs[i], idxs[j] = jnp.where(first, idxs[i], idxs[j]), jnp.where(first, idxs[j], idxs[i])
    for r in range(k):
        m = jnp.max(vals[0], axis=0, keepdims=True)
        im = jnp.min(jnp.where(vals[0] == m, idxs[0], N_KEYS), axis=0, keepdims=True)
        vals_ref[r:r + 1, :] = m
        idx_ref[r:r + 1, :] = im
        taken = idxs[0] == im
        for v in range(k - 1 - r):
            vals[v] = jnp.where(taken, vals[v + 1], vals[v])
            idxs[v] = jnp.where(taken, idxs[v + 1], idxs[v])


def _topk_pairs(v0_sc, v1_sc, vals_ref, flat_ref):
    k = PEER_TOPK
    tt = v0_sc.shape[1]
    row8 = lax.broadcasted_iota(jnp.int32, (SUBLANES, tt), 0)
    v0_lo = v0_sc[0:SUBLANES, :]
    lists = [jnp.where(row8 < k // (b + 1), v0_lo + v1_sc[b:b + 1, :], -jnp.inf) for b in range(k)]
    single = v0_sc[SUBLANES:k, :] + v1_sc[0:1, :]
    pos = jnp.zeros((SUBLANES, tt), jnp.int32)
    flat_single = (row8 + SUBLANES) * k
    for r in range(k):
        flat_head = row8 * k + pos
        m = jnp.max(jnp.maximum(lists[0], single), axis=0, keepdims=True)
        best = jnp.min(jnp.minimum(jnp.where(lists[0] == m, flat_head, k * k),
                                   jnp.where(single == m, flat_single, k * k)), axis=0, keepdims=True)
        vals_ref[r:r + 1, :] = m
        flat_ref[r:r + 1, :] = best
        taken = flat_head == best
        single = jnp.where(flat_single == best, -jnp.inf, single)
        pos = pos + taken.astype(jnp.int32)
        for v in range(k - 1 - r):
            lists[v] = jnp.where(taken, lists[v + 1], lists[v])


def _peer_route_kernel(pq_ref, sk_ref, i1_ref, i2_ref, g_ref, v0_sc, i0_sc, v1_sc, j1_sc, bs_sc, bc_sc):
    k = PEER_TOPK
    for c, (v_sc, i_sc) in enumerate(((v0_sc, i0_sc), (v1_sc, j1_sc))):
        s_t = lax.dot_general(sk_ref[c], pq_ref[c].astype(BF16), NT_DIMS, preferred_element_type=F32)
        _topk_keys(s_t, k, v_sc, i_sc)
    _topk_pairs(v0_sc, v1_sc, bs_sc, bc_sc)
    bs = bs_sc[...]
    bc = bc_sc[...]
    row = lax.broadcasted_iota(jnp.int32, bs.shape, 0)
    i0 = i0_sc[...]
    j1 = j1_sc[...]
    for r in range(k):
        ca = bc[r:r + 1, :] >> 4
        cb = bc[r:r + 1, :] & (k - 1)
        i1_ref[r:r + 1, :] = jnp.sum(jnp.where(row == ca, i0, 0), axis=0, keepdims=True)
        i2_ref[r:r + 1, :] = jnp.sum(jnp.where(row == cb, j1, 0), axis=0, keepdims=True)
    e = jnp.exp(bs - bs[0:1, :])
    g_ref[...] = e / jnp.sum(e, axis=0, keepdims=True)


def peer_route(pq, subkeys, *, tt=256):
    s = pq.shape[1]
    tt = min(tt, s)
    k = PEER_TOPK
    out = jax.ShapeDtypeStruct((PEER_HEADS * k, s), jnp.int32)
    ospec = pl.BlockSpec((k, tt), lambda i, h: (h, i))
    return pl.pallas_call(
        _peer_route_kernel,
        grid=(s // tt, PEER_HEADS),
        in_specs=[pl.BlockSpec((2, tt, PEER_HALF), lambda i, h: (h, i, 0)),
                  pl.BlockSpec((2, N_KEYS, PEER_HALF), lambda i, h: (h, 0, 0))],
        out_specs=[ospec, ospec, ospec],
        out_shape=[out, out, jax.ShapeDtypeStruct((PEER_HEADS * k, s), F32)],
        scratch_shapes=[pltpu.VMEM((k, tt), F32), pltpu.VMEM((k, tt), jnp.int32),
                        pltpu.VMEM((k, tt), F32), pltpu.VMEM((k, tt), jnp.int32),
                        pltpu.VMEM((k, tt), F32), pltpu.VMEM((k, tt), jnp.int32)],
        compiler_params=_cparams("parallel", "parallel"),
        name="peer_route",
    )(pq, subkeys)


TABLE_GROUP = 16
TABLE_PITCH = N_KEYS + SUBLANES
TABLE_GROUPS_PER_STEP = 4


def _peer_table_kernel(i1t_ref, i2t_ref, gt_ref, w_ref, i1_sc, i2_sc, g_sc, stage_sc, *, tw):
    i1_sc[...] = i1t_ref[...].T
    i2_sc[...] = i2t_ref[...].T
    g_sc[...] = gt_ref[...].T
    sub = lax.broadcasted_iota(jnp.int32, (N_KEYS, PEER_HEADS * PEER_TOPK), 0)

    def groups(gi, carry):
        for sg in range(TABLE_GROUPS_PER_STEP):
            t0 = pl.multiple_of((gi * TABLE_GROUPS_PER_STEP + sg) * TABLE_GROUP, TABLE_GROUP)
            base = sg * TABLE_GROUP * TABLE_PITCH
            for u in range(TABLE_GROUP):
                r1 = i1_sc[pl.ds(t0 + u, 1), :]
                r2 = i2_sc[pl.ds(t0 + u, 1), :]
                gg = g_sc[pl.ds(t0 + u, 1), :]
                a = jnp.where(sub == r1, gg, 0.0).astype(BF16)
                b = jnp.where(sub == r2, 1.0, 0.0).astype(BF16)
                row0 = base + u * TABLE_PITCH
                stage_sc[row0:row0 + N_KEYS, :] = lax.dot_general(
                    a, b, NT_DIMS, preferred_element_type=F32)
            for k1 in range(N_KEYS):
                blk = stage_sc[pl.ds(base + k1, TABLE_GROUP, stride=TABLE_PITCH), :]
                w_ref[pl.ds(t0, TABLE_GROUP), k1 * N_KEYS:(k1 + 1) * N_KEYS] = blk.astype(w_ref.dtype)
        return carry

    lax.fori_loop(0, tw // (TABLE_GROUP * TABLE_GROUPS_PER_STEP), groups, 0)


def peer_table(i1t, i2t, gt, *, tw=256):
    hk, s = i1t.shape
    tw = min(tw, s)
    assert tw % (TABLE_GROUP * TABLE_GROUPS_PER_STEP) == 0
    ispec = pl.BlockSpec((hk, tw), lambda i: (0, i))
    return pl.pallas_call(
        functools.partial(_peer_table_kernel, tw=tw),
        grid=(s // tw,),
        in_specs=[ispec, ispec, ispec],
        out_specs=pl.BlockSpec((tw, N_KEYS * N_KEYS), lambda i: (i, 0)),
        out_shape=jax.ShapeDtypeStruct((s, N_KEYS * N_KEYS), BF16),
        scratch_shapes=[pltpu.VMEM((tw, hk), jnp.int32), pltpu.VMEM((tw, hk), jnp.int32),
                        pltpu.VMEM((tw, hk), F32),
                        pltpu.VMEM((TABLE_GROUPS_PER_STEP * TABLE_GROUP * TABLE_PITCH, N_KEYS), F32)],
        compiler_params=_cparams("parallel"),
        name="peer_table",
    )(i1t, i2t, gt)


def _peer_experts_kernel(n_ref, w_ref, u_ref, v_ref, o_ref):
    j = pl.program_id(1)

    @pl.when(j == 0)
    def _():
        o_ref[...] = jnp.zeros_like(o_ref)

    act = lax.dot_general(n_ref[...], u_ref[...].astype(BF16), NT_DIMS, preferred_element_type=F32)
    m = (jax.nn.gelu(act) * w_ref[...].astype(F32)).astype(BF16)
    o_ref[...] += jnp.dot(m, v_ref[...].astype(BF16), preferred_element_type=F32)


def peer_experts(n2, w_flat, u, v, *, tm=1024, te=1024):
    s, d = n2.shape
    e = u.shape[0]
    tm = min(tm, s)
    once = pl.Buffered(1)
    return pl.pallas_call(
        _peer_experts_kernel,
        grid=(s // tm, e // te),
        in_specs=[
            pl.BlockSpec((tm, d), lambda i, j: (i, 0), pipeline_mode=once),
            pl.BlockSpec((tm, te), lambda i, j: (i, j)),
            pl.BlockSpec((te, d), lambda i, j: (j, 0)),
            pl.BlockSpec((te, d), lambda i, j: (j, 0)),
        ],
        out_specs=pl.BlockSpec((tm, d), lambda i, j: (i, 0), pipeline_mode=once),
        out_shape=jax.ShapeDtypeStruct((s, d), F32),
        compiler_params=_cparams("parallel", "arbitrary"),
        name="peer_experts",
    )(n2, w_flat, u, v)


def _ple_kernel(h_ref, f_ref, p_ref, gple_ref, wg_ref, wp_ref, gfin_ref, o_ref, *, final):
    h = h_ref[...] + f_ref[...]
    n3 = _rms(h, gple_ref[...]).astype(BF16)
    gate = jax.nn.sigmoid(jnp.dot(n3, wg_ref[...].astype(BF16), preferred_element_type=F32))
    pp = jnp.dot(p_ref[...].astype(BF16), wp_ref[...].astype(BF16), preferred_element_type=F32)
    h = h + gate * pp
    if final:
        h = _rms(h, gfin_ref[...])
    o_ref[...] = h


def ple_block(h1, ffn_out, p, ple_norm, w_gate, w_proj, final_norm, *, final, tm=256):
    s, d = h1.shape
    pd = p.shape[1]
    tm = min(tm, s)
    full = lambda i: (0, 0)
    return pl.pallas_call(
        functools.partial(_ple_kernel, final=final),
        grid=(s // tm,),
        in_specs=[
            pl.BlockSpec((tm, d), lambda i: (i, 0)),
            pl.BlockSpec((tm, d), lambda i: (i, 0)),
            pl.BlockSpec((tm, pd), lambda i: (i, 0)),
            pl.BlockSpec((1, d), full),
            pl.BlockSpec((d, d), full, pipeline_mode=pl.Buffered(1)),
            pl.BlockSpec((pd, d), full, pipeline_mode=pl.Buffered(1)),
            pl.BlockSpec((1, d), full),
        ],
        out_specs=pl.BlockSpec((tm, d), lambda i: (i, 0)),
        out_shape=jax.ShapeDtypeStruct((s, d), F32),
        compiler_params=_cparams("parallel"),
        name="ple_block",
    )(h1, ffn_out, p, ple_norm.reshape(1, d), w_gate, w_proj, final_norm.reshape(1, d))


def _rot_cols(w):
    half = QK_ROPE_DIM // 2
    return jnp.concatenate([-w[..., half:], w[..., :half]], axis=-1)


def _layer(h, p, cos128, sin128, prm, *, final, final_norm):
    s, d = h.shape
    (attn_norm, w_in, b_gate, q_norm, w_uq, kv_norm, w_ukv, w_attn_o, conv_w, conv_b, w_rg_a, b_rg_a,
     w_rg_x, b_rg_x, lru_lambda, w_rnn_o, w_out, ffn_norm, w_peer_q, peer_subkeys, peer_u, peer_v,
     ple_norm, w_ple_gate, w_ple_proj) = prm

    o_q, o_kv, o_kr, o_x, o_y, o_g = 0, Q_LORA_RANK, Q_LORA_RANK + KV_LORA_RANK, \
        Q_LORA_RANK + KV_LORA_RANK + QK_ROPE_DIM, Q_LORA_RANK + KV_LORA_RANK + QK_ROPE_DIM + d, \
        Q_LORA_RANK + KV_LORA_RANK + QK_ROPE_DIM + 2 * d
    w_in_t = w_in.T
    w_kr = w_in_t[o_kr:o_x]
    w_kr_rot = _rot_cols(w_kr.T).T
    w_lat = jnp.concatenate([w_in_t[o_q:o_kr], w_kr, w_kr, w_kr_rot, w_kr_rot], axis=0).astype(BF16)
    uq = w_uq.reshape(Q_LORA_RANK, N_HEADS, QK_HEAD_DIM)
    uq_pe = uq[:, :, QK_NOPE_DIM:]
    w_q = jnp.concatenate([uq[:, :, :QK_NOPE_DIM].reshape(Q_LORA_RANK, -1),
                           uq_pe.reshape(Q_LORA_RANK, -1),
                           _rot_cols(uq_pe).reshape(Q_LORA_RANK, -1)], axis=1).astype(BF16)
    n_lat = Q_LORA_RANK + KV_LORA_RANK + 2 * LANES
    n_qn = N_HEADS * QK_NOPE_DIM
    n_qp = N_HEADS * QK_ROPE_DIM
    scale = QK_HEAD_DIM ** -0.5 * math.log2(math.e)

    n1 = rmsnorm_bf16(h, attn_norm)

    def lat_body(acc, extra, outs):
        outs[0][...] = acc[:, :Q_LORA_RANK]
        outs[1][...] = acc[:, Q_LORA_RANK:Q_LORA_RANK + KV_LORA_RANK]
        outs[2][...] = acc[:, Q_LORA_RANK + KV_LORA_RANK:]

    tm_lat = 512
    c_q, c_kv, kr4 = _mm_call(
        "in_proj_latents", n1, w_lat, tm=tm_lat, tn=n_lat, body=lat_body, w_rows_are_outputs=True,
        out_shapes=[jax.ShapeDtypeStruct((s, Q_LORA_RANK), F32), jax.ShapeDtypeStruct((s, KV_LORA_RANK), F32),
                    jax.ShapeDtypeStruct((s, 2 * LANES), F32)],
        out_specs=[pl.BlockSpec((min(tm_lat, s), Q_LORA_RANK), lambda j, i: (i, 0)),
                   pl.BlockSpec((min(tm_lat, s), KV_LORA_RANK), lambda j, i: (i, 0)),
                   pl.BlockSpec((min(tm_lat, s), 2 * LANES), lambda j, i: (i, 0))])

    tm, tn = min(1024, s), 1024

    def store_body(acc, extra, outs):
        outs[0][...] = acc.astype(outs[0].dtype)

    x_r = _mm_call("in_proj_x", n1, w_in_t, tm=tm, tn=tn, body=store_body, w_rows_are_outputs=True,
                   w_row_range=(o_x, o_y),
                   out_shapes=jax.ShapeDtypeStruct((s, d), F32), out_specs=_tile_spec(tm, tn))

    def gelu_body(acc, extra, outs):
        outs[0][...] = jax.nn.gelu(acc).astype(outs[0].dtype)

    gy = _mm_call("in_proj_y", n1, w_in_t, tm=tm, tn=tn, body=gelu_body, w_rows_are_outputs=True,
                  w_row_range=(o_y, o_g),
                  out_shapes=jax.ShapeDtypeStruct((s, d), BF16), out_specs=_tile_spec(tm, tn))

    def gate_body(acc, extra, outs):
        outs[0][...] = jax.nn.sigmoid(acc + extra[0][...]).astype(outs[0].dtype)

    gates = _mm_call("in_proj_gates", n1, w_in_t, tm=tm, tn=tn, body=gate_body, w_rows_are_outputs=True,
                     w_row_range=(o_g, o_g + 2 * d),
                     extras=(b_gate.reshape(1, 2 * d),), extra_specs=(_col_spec(tn),),
                     out_shapes=jax.ShapeDtypeStruct((s, 2 * d), BF16), out_specs=_tile_spec(tm, tn))

    def norm_prologue(a, extra):
        return _rms(a, extra[0][...]).astype(BF16)

    def q_body(acc, extra, outs):
        outs[0][...] = (acc[:, :n_qn] * scale).astype(BF16)
        cos_t = jnp.tile(extra[1][...], (1, n_qp // LANES))
        sin_t = jnp.tile(extra[2][...], (1, n_qp // LANES))
        pe = acc[:, n_qn:n_qn + n_qp] * cos_t + acc[:, n_qn + n_qp:] * sin_t
        outs[1][...] = (pe * scale).astype(BF16)

    tm_q = min(512, s)
    row128 = pl.BlockSpec((tm_q, LANES), lambda j, i: (i, 0))
    qn, qpe = _mm_call(
        "q_proj", c_q, w_q, tm=tm_q, tn=n_qn + 2 * n_qp, prologue=norm_prologue, body=q_body,
        extras=(q_norm.reshape(1, Q_LORA_RANK), cos128, sin128),
        extra_specs=(pl.BlockSpec((1, Q_LORA_RANK), lambda j, i: (0, 0)), row128, row128),
        out_shapes=[jax.ShapeDtypeStruct((s, n_qn), BF16), jax.ShapeDtypeStruct((s, n_qp), BF16)],
        out_specs=[pl.BlockSpec((tm_q, n_qn), lambda j, i: (i, 0)), pl.BlockSpec((tm_q, n_qp), lambda j, i: (i, 0))])

    def kv_body(acc, extra, outs):
        outs[0][...] = acc.astype(BF16)
        kr = extra[1][...]
        kpe2 = kr[:, :LANES] * extra[2][...] + kr[:, LANES:] * extra[3][...]
        lane = lax.broadcasted_iota(jnp.int32, kpe2.shape, 1)
        outs[1][...] = jnp.concatenate([jnp.where(lane < QK_ROPE_DIM, kpe2, 0.0),
                                        jnp.where(lane >= QK_ROPE_DIM, kpe2, 0.0)], axis=1).astype(BF16)

    n_kv = N_HEADS * (QK_NOPE_DIM + V_HEAD_DIM)
    kv, kpe = _mm_call(
        "kv_proj", c_kv, w_ukv, tm=tm_q, tn=n_kv, prologue=norm_prologue, body=kv_body,
        extras=(kv_norm.reshape(1, KV_LORA_RANK), kr4, cos128, sin128),
        extra_specs=(pl.BlockSpec((1, KV_LORA_RANK), lambda j, i: (0, 0)),
                     pl.BlockSpec((tm_q, 2 * LANES), lambda j, i: (i, 0)), row128, row128),
        out_shapes=[jax.ShapeDtypeStruct((s, n_kv), BF16), jax.ShapeDtypeStruct((s, 2 * LANES), BF16)],
        out_specs=[pl.BlockSpec((tm_q, n_kv), lambda j, i: (i, 0)), pl.BlockSpec((tm_q, 2 * LANES), lambda j, i: (i, 0))])

    o_attn = mla_attention(qn, qpe, kv, kpe)
    y_rnn_in = rglru(x_r, gy, conv_w, conv_b, w_rg_a, b_rg_a, w_rg_x, b_rg_x, lru_lambda)

    def merge_kernel(o_ref, y_ref, wo_ref, wr_ref, ga_ref, gr_ref, m_ref):
        ya = jnp.dot(o_ref[...], wo_ref[...], preferred_element_type=F32)
        yr = jnp.dot(y_ref[...], wr_ref[...], preferred_element_type=F32)
        m_ref[...] = (ga_ref[...].astype(F32) * ya + gr_ref[...].astype(F32) * yr).astype(m_ref.dtype)

    a_spec = pl.BlockSpec((tm, d), lambda j, i: (i, 0))
    w_spec = pl.BlockSpec((d, tn), lambda j, i: (0, j))
    merged = pl.pallas_call(
        merge_kernel,
        grid=(d // tn, s // tm),
        in_specs=[a_spec, a_spec, w_spec, w_spec, _tile_spec(tm, tn), _tile_spec(tm, tn, d // tn)],
        out_specs=_tile_spec(tm, tn),
        out_shape=jax.ShapeDtypeStruct((s, d), BF16),
        compiler_params=_cparams("parallel", "parallel"),
        name="mixer_merge",
    )(o_attn, y_rnn_in, w_attn_o.astype(BF16), w_rnn_o.astype(BF16), gates, gates)

    def out_body(acc, extra, outs):
        h1 = extra[0][...] + acc
        outs[0][...] = h1
        outs[1][...] = _rms(h1, extra[1][...]).astype(BF16)

    tm_o = min(512, s)
    rowd = pl.BlockSpec((tm_o, d), lambda j, i: (i, 0))
    h1, n2 = _mm_call(
        "out_proj", merged, w_out, tm=tm_o, tn=d, body=out_body,
        extras=(h, ffn_norm.reshape(1, d)), extra_specs=(rowd, pl.BlockSpec((1, d), lambda j, i: (0, 0))),
        out_shapes=[jax.ShapeDtypeStruct((s, d), F32), jax.ShapeDtypeStruct((s, d), BF16)],
        out_specs=[rowd, rowd])

    n_hc = 2 * PEER_HEADS

    def pq_body(acc, extra, outs):
        for c in range(n_hc):
            outs[0][c] = acc[:, c * PEER_HALF:(c + 1) * PEER_HALF]

    pq = _mm_call("peer_query", n2, w_peer_q, tm=tm_o, tn=n_hc * PEER_HALF, body=pq_body,
                  out_shapes=jax.ShapeDtypeStruct((n_hc, s, PEER_HALF), F32),
                  out_specs=pl.BlockSpec((n_hc, tm_o, PEER_HALF), lambda j, i: (0, i, 0)))
    i1t, i2t, gt = peer_route(pq, peer_subkeys.reshape(n_hc, N_KEYS, PEER_HALF).astype(BF16))
    w_tab = peer_table(i1t, i2t, gt)
    ffn_out = peer_experts(n2, w_tab, peer_u, peer_v)

    return ple_block(h1, ffn_out, p, ple_norm, w_ple_gate, w_ple_proj,
                     final_norm, final=final)


def kernel(x, p, attn_norm, w_in, b_gate, q_norm, w_uq, kv_norm, w_ukv, w_attn_o, conv_w, conv_b, w_rg_a, b_rg_a,
           w_rg_x, b_rg_x, lru_lambda, w_rnn_o, w_out, ffn_norm, w_peer_q, peer_subkeys, peer_u, peer_v, ple_norm,
           w_ple_gate, w_ple_proj, final_norm):
    bsz, s, d = x.shape
    assert bsz == 1, "one sequence per call"
    depth = w_in.shape[0]
    layer_params = (attn_norm, w_in, b_gate, q_norm, w_uq, kv_norm, w_ukv, w_attn_o, conv_w, conv_b, w_rg_a,
                    b_rg_a, w_rg_x, b_rg_x, lru_lambda, w_rnn_o, w_out, ffn_norm, w_peer_q, peer_subkeys, peer_u,
                    peer_v, ple_norm, w_ple_gate, w_ple_proj)

    half = QK_ROPE_DIM // 2
    inv_freq = 1.0 / (ROPE_THETA ** (jnp.arange(half, dtype=F32) / half))
    ang = jnp.arange(s, dtype=F32)[:, None] * inv_freq[None, :]
    cos128 = jnp.tile(jnp.cos(ang), (1, LANES // half))
    sin128 = jnp.tile(jnp.sin(ang), (1, LANES // half))

    h = x.reshape(s, d)
    for l in range(depth):
        h = _layer(h, p[l, 0], cos128, sin128, tuple(w[l] for w in layer_params),
                   final=(l == depth - 1), final_norm=final_norm)
    return h.reshape(bsz, s, d)
```

```python
import functools
import math

import jax
import jax.numpy as jnp
from jax import lax
from jax.experimental import pallas as pl
from jax.experimental.pallas import tpu as pltpu

RMS_EPS = 1e-6
N_HEADS = 16
QK_NOPE_DIM = 128
QK_ROPE_DIM = 64
V_HEAD_DIM = 128
QK_HEAD_DIM = QK_NOPE_DIM + QK_ROPE_DIM
Q_LORA_RANK = 768
KV_LORA_RANK = 512
ROPE_THETA = 10000.0
RNN_BLOCKS = 16
RNN_BLOCK_DIM = 128
CONV_WIDTH = 4
LRU_C = 8.0
PEER_HEADS = 8
N_KEYS = 128
PEER_HALF = 128
PEER_TOPK = 16

LANES = 128
SUBLANES = 8
VMEM_LIMIT_BYTES = 56 * 1024 * 1024

F32 = jnp.float32
BF16 = jnp.bfloat16
NT_DIMS = (((1,), (1,)), ((), ()))


def _cparams(*sem):
    return pltpu.CompilerParams(dimension_semantics=sem, vmem_limit_bytes=VMEM_LIMIT_BYTES)


def _rms(x, g):
    ms = jnp.mean(x * x, axis=-1, keepdims=True)
    return (x * lax.rsqrt(ms + RMS_EPS)) * g


def _rmsnorm_kernel(x_ref, g_ref, o_ref):
    o_ref[...] = _rms(x_ref[...], g_ref[...]).astype(o_ref.dtype)


def rmsnorm_bf16(x, g, *, tr=512):
    m, d = x.shape
    tr = min(tr, m)
    return pl.pallas_call(
        _rmsnorm_kernel,
        grid=(m // tr,),
        in_specs=[pl.BlockSpec((tr, d), lambda i: (i, 0)), pl.BlockSpec((1, d), lambda i: (0, 0))],
        out_specs=pl.BlockSpec((tr, d), lambda i: (i, 0)),
        out_shape=jax.ShapeDtypeStruct((m, d), BF16),
        compiler_params=_cparams("parallel"),
        name="rmsnorm",
    )(x, g.reshape(1, d))


def _mm_kernel(a_ref, w_ref, *refs, n_extra, prologue, body, w_rows_are_outputs):
    extra = refs[:n_extra]
    outs = refs[n_extra:]
    a = a_ref[...]
    if prologue is not None:
        a = prologue(a, extra)
    w = w_ref[...].astype(BF16)
    if w_rows_are_outputs:
        acc = lax.dot_general(a, w, NT_DIMS, preferred_element_type=F32)
    else:
        acc = jnp.dot(a, w, preferred_element_type=F32)
    body(acc, extra, outs)


def _mm_call(name, a, w, *, tm, tn, extras=(), extra_specs=(), out_shapes, out_specs, body, prologue=None,
             w_rows_are_outputs=False, w_row_range=None):
    m, k = a.shape
    tm = min(tm, m)
    if w_row_range is not None:
        assert w_rows_are_outputs
        row0, row1 = w_row_range
        n = row1 - row0
        assert row0 % SUBLANES == 0 and w.dtype == F32
        w_spec = pl.BlockSpec((pl.Element(tn), pl.Element(k)),
                              lambda j, i: (pl.multiple_of(row0 + j * tn, SUBLANES), 0))
    else:
        n = w.shape[0] if w_rows_are_outputs else w.shape[1]
        mode = dict(pipeline_mode=pl.Buffered(1)) if n == tn else {}
        w_spec = pl.BlockSpec((tn, k), lambda j, i: (j, 0), **mode) if w_rows_are_outputs else \
            pl.BlockSpec((k, tn), lambda j, i: (0, j), **mode)
    assert m % tm == 0 and n % tn == 0
    kern = functools.partial(_mm_kernel, n_extra=len(extras), prologue=prologue, body=body,
                             w_rows_are_outputs=w_rows_are_outputs)
    return pl.pallas_call(
        kern,
        grid=(n // tn, m // tm),
        in_specs=[pl.BlockSpec((tm, k), lambda j, i: (i, 0)), w_spec]
        + list(extra_specs),
        out_specs=out_specs,
        out_shape=out_shapes,
        compiler_params=_cparams("parallel", "parallel"),
        name=name,
    )(a, w, *extras)


def _tile_spec(tm, tn, col_off_blocks=0):
    return pl.BlockSpec((tm, tn), lambda j, i: (i, j + col_off_blocks))


def _col_spec(tn):
    return pl.BlockSpec((1, tn), lambda j, i: (0, j))


ATTN_ROW_CHUNK = 32


def _attn_kernel(qn_ref, qpe_ref, kv_ref, kpe_ref, o_ref, s0_sc, s1_sc, p0_sc, p1_sc, a0_sc, a1_sc,
                 m_sc, l_sc, acc_sc, *, tq):
    i = pl.program_id(1)
    rc = ATTN_ROW_CHUNK
    qpe = qpe_ref[...]
    qs = [jnp.concatenate([qn_ref[:, hh * LANES:(hh + 1) * LANES], qpe], axis=1) for hh in range(2)]
    m_sc[...] = jnp.full(m_sc.shape, -jnp.inf, F32)
    l_sc[...] = jnp.zeros(l_sc.shape, F32)
    acc_sc[...] = jnp.zeros(acc_sc.shape, F32)
    chunks = [slice(c * rc, (c + 1) * rc) for c in range(tq // rc)]
    even = (s0_sc, p0_sc, a0_sc)
    odd = (s1_sc, p1_sc, a1_sc)

    def scores(kb, bufs):
        s_sc = bufs[0]
        start = pl.multiple_of(kb * tq, tq)
        for hh in range(2):
            k = jnp.concatenate(
                [kv_ref[pl.ds(start, tq), hh * 256:hh * 256 + LANES],
                 kpe_ref[pl.ds(start, tq), hh * LANES:(hh + 1) * LANES]], axis=1)
            s_sc[hh] = lax.dot_general(qs[hh], k, NT_DIMS, preferred_element_type=F32)

    def mask_diagonal(bufs):
        s_sc = bufs[0]
        for hh in range(2):
            for rows in chunks:
                row = lax.broadcasted_iota(jnp.int32, (rc, tq), 0) + rows.start
                col = lax.broadcasted_iota(jnp.int32, (rc, tq), 1)
                s_sc[hh, rows, :] = jnp.where(col <= row, s_sc[hh, rows, :], -jnp.inf)

    def accumulate(kb, bufs):
        s_sc, p_sc, a_sc = bufs
        start = pl.multiple_of(kb * tq, tq)
        for hh in range(2):
            for rows in chunks:
                m_old = m_sc[hh, rows, :]
                m_new = jnp.maximum(m_old, jnp.max(s_sc[hh, rows, :], axis=-1, keepdims=True))
                a_sc[hh, rows, :] = jnp.exp2(m_old - m_new)
                m_sc[hh, rows, :] = m_new
            for rows in chunks:
                m_new = jnp.concatenate([m_sc[hh, rows, :]] * (tq // LANES), axis=1)
                p = jnp.exp2(s_sc[hh, rows, :] - m_new)
                l_sc[hh, rows, :] = a_sc[hh, rows, :] * l_sc[hh, rows, :] + jnp.sum(p, axis=-1, keepdims=True)
                p_sc[hh, rows, :] = p.astype(BF16)
        for hh in range(2):
            v = kv_ref[pl.ds(start, tq), hh * 256 + LANES:(hh + 1) * 256]
            acc_sc[hh] = a_sc[hh] * acc_sc[hh] + jnp.dot(p_sc[hh], v, preferred_element_type=F32)

    scores(0, even)

    def two_blocks(kk, carry):
        kb = 2 * kk + 1
        scores(kb, odd)
        accumulate(kb - 1, even)
        scores(kb + 1, even)
        accumulate(kb, odd)
        return carry

    lax.fori_loop(0, i // 2, two_blocks, 0)

    @pl.when(i % 2 == 1)
    def _():
        scores(i, odd)
        accumulate(i - 1, even)
        mask_diagonal(odd)
        accumulate(i, odd)

    @pl.when(i % 2 == 0)
    def _():
        mask_diagonal(even)
        accumulate(i, even)

    for hh in range(2):
        o_ref[:, hh * LANES:(hh + 1) * LANES] = (acc_sc[hh] / l_sc[hh]).astype(o_ref.dtype)


def mla_attention(qn, qpe, kv, kpe, *, tq=512):
    s = qn.shape[0]
    tq = min(tq, s)
    n_pairs = N_HEADS // 2
    return pl.pallas_call(
        functools.partial(_attn_kernel, tq=tq),
        grid=(n_pairs, s // tq),
        in_specs=[
            pl.BlockSpec((tq, 2 * LANES), lambda j, i: (i, j)),
            pl.BlockSpec((tq, LANES), lambda j, i: (i, j)),
            pl.BlockSpec((s, 4 * LANES), lambda j, i: (0, j)),
            pl.BlockSpec((s, 2 * LANES), lambda j, i: (0, 0)),
        ],
        out_specs=pl.BlockSpec((tq, 2 * LANES), lambda j, i: (i, j)),
        out_shape=jax.ShapeDtypeStruct((s, N_HEADS * V_HEAD_DIM), BF16),
        scratch_shapes=[pltpu.VMEM((2, tq, tq), F32), pltpu.VMEM((2, tq, tq), F32),
                        pltpu.VMEM((2, tq, tq), BF16), pltpu.VMEM((2, tq, tq), BF16),
                        pltpu.VMEM((2, tq, LANES), F32), pltpu.VMEM((2, tq, LANES), F32),
                        pltpu.VMEM((2, tq, LANES), F32), pltpu.VMEM((2, tq, LANES), F32),
                        pltpu.VMEM((2, tq, V_HEAD_DIM), F32)],
        compiler_params=_cparams("parallel", "parallel"),
        name="mla_attention",
    )(qn, qpe, kv, kpe)


def _softplus(z):
    return jnp.maximum(z, 0.0) + jnp.log1p(jnp.exp(-jnp.abs(z)))


RGLRU_BLOCKS_PER_STEP = 4


def _rglru_kernel(x_ref, gy_ref, cw_ref, cb_ref, wa_ref, ba_ref, wx_ref, bx_ref, lam_ref, y_ref,
                  h_sc, edge_sc, a_sc, b_sc, *, ts):
    tb = pl.program_id(1)

    @pl.when(tb == 0)
    def _():
        h_sc[...] = jnp.zeros_like(h_sc)
        edge_sc[0:SUBLANES, :] = jnp.zeros((SUBLANES, edge_sc.shape[1]), F32)
        a_sc[0:SUBLANES, :] = jnp.ones((SUBLANES, a_sc.shape[1]), F32)
        b_sc[0:SUBLANES, :] = jnp.zeros((SUBLANES, b_sc.shape[1]), F32)

    row8 = lax.broadcasted_iota(jnp.int32, (SUBLANES, LANES), 0)
    for c in range(RGLRU_BLOCKS_PER_STEP):
        ln = slice(c * LANES, (c + 1) * LANES)
        x = x_ref[:, ln]
        cw = cw_ref[:, ln]
        edge_sc[SUBLANES:2 * SUBLANES, ln] = x[0:SUBLANES, :]
        xc = x * cw[CONV_WIDTH - 1:CONV_WIDTH, :] + cb_ref[:, ln]
        for d in range(1, CONV_WIDTH):
            xs = jnp.concatenate([edge_sc[SUBLANES - d:2 * SUBLANES - d, ln], x_ref[SUBLANES - d:ts - d, ln]], axis=0)
            xc = xc + xs * cw[CONV_WIDTH - 1 - d:CONV_WIDTH - d, :]
        edge_sc[0:SUBLANES, ln] = x[ts - SUBLANES:, :]

        xcb = xc.astype(BF16)
        r = jax.nn.sigmoid(jnp.dot(xcb, wa_ref[c], preferred_element_type=F32) + ba_ref[c])
        ig = jax.nn.sigmoid(jnp.dot(xcb, wx_ref[c], preferred_element_type=F32) + bx_ref[c])
        log_a = (-LRU_C) * r * _softplus(-lam_ref[:, ln])
        a = jnp.exp(log_a)
        th = jnp.tanh(log_a)
        m2 = -2.0 * th / (1.0 - th)
        mult = jnp.where(m2 > 0.0, m2 * lax.rsqrt(m2), 0.0)
        gated = ig * xc
        b = mult * gated
        first = jnp.where((row8 == 0) & (tb == 0), gated[0:SUBLANES, :], b[0:SUBLANES, :])
        b = jnp.concatenate([first, b[SUBLANES:, :]], axis=0)

        d = 1
        while d < ts:
            if d < SUBLANES:
                a_sc[SUBLANES:, ln] = a
                b_sc[SUBLANES:, ln] = b
                a_sh = a_sc[SUBLANES - d:SUBLANES - d + ts, ln]
                b_sh = b_sc[SUBLANES - d:SUBLANES - d + ts, ln]
            else:
                a_sh = jnp.concatenate([jnp.ones((d, LANES), F32), a[:ts - d]], axis=0)
                b_sh = jnp.concatenate([jnp.zeros((d, LANES), F32), b[:ts - d]], axis=0)
            b = a * b_sh + b
            a = a * a_sh
            d *= 2
        h = a * h_sc[0:1, ln] + b
        h_sc[:, ln] = jnp.broadcast_to(h[ts - 1:ts, :], (SUBLANES, LANES))
        y_ref[:, ln] = (h * gy_ref[:, ln].astype(F32)).astype(y_ref.dtype)


def rglru(x_r, gy, conv_w, conv_b, w_rg_a, b_rg_a, w_rg_x, b_rg_x, lru_lambda, *, ts=512):
    s, c = x_r.shape
    ts = min(ts, s)
    nblk = RGLRU_BLOCKS_PER_STEP
    wide = nblk * LANES
    n_gate_blocks = c // RNN_BLOCK_DIM
    col = lambda cb, tb: (0, cb)
    blk = lambda cb, tb: (cb, 0, 0)
    return pl.pallas_call(
        functools.partial(_rglru_kernel, ts=ts),
        grid=(c // wide, s // ts),
        in_specs=[
            pl.BlockSpec((ts, wide), lambda cb, tb: (tb, cb)),
            pl.BlockSpec((ts, wide), lambda cb, tb: (tb, cb)),
            pl.BlockSpec((CONV_WIDTH, wide), col),
            pl.BlockSpec((1, wide), col),
            pl.BlockSpec((nblk, RNN_BLOCK_DIM, RNN_BLOCK_DIM), blk),
            pl.BlockSpec((nblk, 1, RNN_BLOCK_DIM), blk),
            pl.BlockSpec((nblk, RNN_BLOCK_DIM, RNN_BLOCK_DIM), blk),
            pl.BlockSpec((nblk, 1, RNN_BLOCK_DIM), blk),
            pl.BlockSpec((1, wide), col),
        ],
        out_specs=pl.BlockSpec((ts, wide), lambda cb, tb: (tb, cb)),
        out_shape=jax.ShapeDtypeStruct((s, c), BF16),
        scratch_shapes=[pltpu.VMEM((SUBLANES, wide), F32), pltpu.VMEM((2 * SUBLANES, wide), F32),
                        pltpu.VMEM((SUBLANES + ts, wide), F32), pltpu.VMEM((SUBLANES + ts, wide), F32)],
        compiler_params=_cparams("parallel", "arbitrary"),
        name="rglru",
    )(x_r, gy, conv_w, conv_b.reshape(1, c), w_rg_a.astype(BF16), b_rg_a.reshape(n_gate_blocks, 1, RNN_BLOCK_DIM),
      w_rg_x.astype(BF16), b_rg_x.reshape(n_gate_blocks, 1, RNN_BLOCK_DIM), lru_lambda.reshape(1, c))


def _sorting_network(n):
    pairs = []

    def merge(lo, length, r):
        step = r * 2
        if step < length:
            merge(lo, length, step)
            merge(lo + r, length, step)
            pairs.extend((i, i + r) for i in range(lo + r, lo + length - r, step))
        else:
            pairs.append((lo, lo + r))

    def sort(lo, length):
        if length > 1:
            half = length // 2
            sort(lo, half)
            sort(lo + half, half)
            merge(lo, length, 1)

    sort(0, n)
    return pairs


def _topk_keys(s, k, vals_ref, idx_ref):
    n_slabs = s.shape[0] // SUBLANES
    assert n_slabs == k
    sub = lax.broadcasted_iota(jnp.int32, (SUBLANES, s.shape[1]), 0)
    vals = [s[v * SUBLANES:(v + 1) * SUBLANES, :] for v in range(n_slabs)]
    idxs = [sub + v * SUBLANES for v in range(n_slabs)]
    for i, j in _sorting_network(n_slabs):
        first = (vals[i] > vals[j]) | ((vals[i] == vals[j]) & (idxs[i] < idxs[j]))
        vals[i], vals[j] = jnp.where(first, vals[i], vals[j]), jnp.where(first, vals[j], vals[i])
        idxs[i], idxs[j] = jnp.where(first, idxs[i], idxs[j]), jnp.where(first, idxs[j], idxs[i])
    for r in range(k):
        m = jnp.max(vals[0], axis=0, keepdims=True)
        im = jnp.min(jnp.where(vals[0] == m, idxs[0], N_KEYS), axis=0, keepdims=True)
        vals_ref[r:r + 1, :] = m
        idx_ref[r:r + 1, :] = im
        taken = idxs[0] == im
        for v in range(k - 1 - r):
            vals[v] = jnp.where(taken, vals[v + 1], vals[v])
            idxs[v] = jnp.where(taken, idxs[v + 1], idxs[v])


def _topk_pairs(v0_sc, v1_sc, vals_ref, flat_ref):
    k = PEER_TOPK
    tt = v0_sc.shape[1]
    row8 = lax.broadcasted_iota(jnp.int32, (SUBLANES, tt), 0)
    v0_lo = v0_sc[0:SUBLANES, :]
    lists = [jnp.where(row8 < k // (b + 1), v0_lo + v1_sc[b:b + 1, :], -jnp.inf) for b in range(k)]
    single = v0_sc[SUBLANES:k, :] + v1_sc[0:1, :]
    pos = jnp.zeros((SUBLANES, tt), jnp.int32)
    flat_single = (row8 + SUBLANES) * k
    for r in range(k):
        flat_head = row8 * k + pos
        m = jnp.max(jnp.maximum(lists[0], single), axis=0, keepdims=True)
        best = jnp.min(jnp.minimum(jnp.where(lists[0] == m, flat_head, k * k),
                                   jnp.where(single == m, flat_single, k * k)), axis=0, keepdims=True)
        vals_ref[r:r + 1, :] = m
        flat_ref[r:r + 1, :] = best
        taken = flat_head == best
        single = jnp.where(flat_single == best, -jnp.inf, single)
        pos = pos + taken.astype(jnp.int32)
        for v in range(k - 1 - r):
            lists[v] = jnp.where(taken, lists[v + 1], lists[v])


def _peer_route_kernel(pq_ref, sk_ref, i1_ref, i2_ref, g_ref, v0_sc, i0_sc, v1_sc, j1_sc, bs_sc, bc_sc):
    k = PEER_TOPK
    for c, (v_sc, i_sc) in enumerate(((v0_sc, i0_sc), (v1_sc, j1_sc))):
        s_t = lax.dot_general(sk_ref[c], pq_ref[c].astype(BF16), NT_DIMS, preferred_element_type=F32)
        _topk_keys(s_t, k, v_sc, i_sc)
    _topk_pairs(v0_sc, v1_sc, bs_sc, bc_sc)
    bs = bs_sc[...]
    bc = bc_sc[...]
    row = lax.broadcasted_iota(jnp.int32, bs.shape, 0)
    i0 = i0_sc[...]
    j1 = j1_sc[...]
    for r in range(k):
        ca = bc[r:r + 1, :] >> 4
        cb = bc[r:r + 1, :] & (k - 1)
        i1_ref[r:r + 1, :] = jnp.sum(jnp.where(row == ca, i0, 0), axis=0, keepdims=True)
        i2_ref[r:r + 1, :] = jnp.sum(jnp.where(row == cb, j1, 0), axis=0, keepdims=True)
    e = jnp.exp(bs - bs[0:1, :])
    g_ref[...] = e / jnp.sum(e, axis=0, keepdims=True)


TABLE_GROUP = 16
TABLE_PITCH = N_KEYS + SUBLANES
TABLE_GROUPS_PER_STEP = 8


def _peer_route_table_kernel(pq_ref, sk_ref, w_ref, r1_sc, r2_sc, rg_sc, v0_sc, i0_sc, v1_sc, j1_sc, bs_sc, bc_sc,
                             i1_sc, i2_sc, g_sc, stage_sc, *, tw):
    i = pl.program_id(0)

    @pl.when(i == 0)
    def _():
        r1_sc[...] = jnp.zeros_like(r1_sc)
        r2_sc[...] = jnp.zeros_like(r2_sc)
        rg_sc[...] = jnp.zeros_like(rg_sc)

    cur = lax.rem(i, 2)
    prev = 1 - cur
    i1_sc[...] = r1_sc[prev].T
    i2_sc[...] = r2_sc[prev].T
    g_sc[...] = rg_sc[prev].T
    sub = lax.broadcasted_iota(jnp.int32, (N_KEYS, PEER_HEADS * PEER_TOPK), 0)
    n_iter = tw // (TABLE_GROUP * TABLE_GROUPS_PER_STEP)
    heads_per_iter = PEER_HEADS // n_iter
    k = PEER_TOPK

    def body(gi, carry):
        for sg in range(TABLE_GROUPS_PER_STEP):
            t0 = pl.multiple_of((gi * TABLE_GROUPS_PER_STEP + sg) * TABLE_GROUP, TABLE_GROUP)
            base = sg * TABLE_GROUP * TABLE_PITCH
            for u in range(TABLE_GROUP):
                r1 = i1_sc[pl.ds(t0 + u, 1), :]
                r2 = i2_sc[pl.ds(t0 + u, 1), :]
                gg = g_sc[pl.ds(t0 + u, 1), :]
                a = jnp.where(sub == r1, gg, 0.0).astype(BF16)
                b = jnp.where(sub == r2, 1.0, 0.0).astype(BF16)
                row0 = base + u * TABLE_PITCH
                stage_sc[row0:row0 + N_KEYS, :] = lax.dot_general(
                    a, b, NT_DIMS, preferred_element_type=F32)
            for k1 in range(N_KEYS):
                blk = stage_sc[pl.ds(base + k1, TABLE_GROUP, stride=TABLE_PITCH), :]
                w_ref[pl.ds(t0, TABLE_GROUP), k1 * N_KEYS:(k1 + 1) * N_KEYS] = blk.astype(w_ref.dtype)
        for hh in range(heads_per_iter):
            h = gi * heads_per_iter + hh
            rows = pl.ds(pl.multiple_of(h * k, k), k)
            _peer_route_kernel(pq_ref.at[pl.ds(2 * h, 2)], sk_ref.at[pl.ds(2 * h, 2)],
                               r1_sc.at[cur, rows], r2_sc.at[cur, rows], rg_sc.at[cur, rows],
                               v0_sc, i0_sc, v1_sc, j1_sc, bs_sc, bc_sc)
        return carry

    lax.fori_loop(0, n_iter, body, 0)


def peer_route_table(pq, subkeys, *, tw=256):
    s = pq.shape[1]
    tw = min(tw, s)
    n_tiles = s // tw
    k = PEER_TOPK
    hk = PEER_HEADS * k
    assert tw % (TABLE_GROUP * TABLE_GROUPS_PER_STEP) == 0
    assert PEER_HEADS % (tw // (TABLE_GROUP * TABLE_GROUPS_PER_STEP)) == 0
    return pl.pallas_call(
        functools.partial(_peer_route_table_kernel, tw=tw),
        grid=(n_tiles + 1,),
        in_specs=[pl.BlockSpec((2 * PEER_HEADS, tw, PEER_HALF), lambda i: (0, jnp.minimum(i, n_tiles - 1), 0)),
                  pl.BlockSpec((2 * PEER_HEADS, N_KEYS, PEER_HALF), lambda i: (0, 0, 0))],
        out_specs=pl.BlockSpec((tw, N_KEYS * N_KEYS), lambda i: (jnp.maximum(i - 1, 0), 0)),
        out_shape=jax.ShapeDtypeStruct((s, N_KEYS * N_KEYS), BF16),
        scratch_shapes=[pltpu.VMEM((2, hk, tw), jnp.int32), pltpu.VMEM((2, hk, tw), jnp.int32),
                        pltpu.VMEM((2, hk, tw), F32),
                        pltpu.VMEM((k, tw), F32), pltpu.VMEM((k, tw), jnp.int32),
                        pltpu.VMEM((k, tw), F32), pltpu.VMEM((k, tw), jnp.int32),
                        pltpu.VMEM((k, tw), F32), pltpu.VMEM((k, tw), jnp.int32),
                        pltpu.VMEM((tw, hk), jnp.int32), pltpu.VMEM((tw, hk), jnp.int32), pltpu.VMEM((tw, hk), F32),
                        pltpu.VMEM((TABLE_GROUPS_PER_STEP * TABLE_GROUP * TABLE_PITCH, N_KEYS), F32)],
        compiler_params=_cparams("arbitrary"),
        name="peer_route_table",
    )(pq, subkeys)


def _peer_experts_kernel(n_ref, w_ref, u_ref, v_ref, o_ref):
    j = pl.program_id(1)

    @pl.when(j == 0)
    def _():
        o_ref[...] = jnp.zeros_like(o_ref)

    act = lax.dot_general(n_ref[...], u_ref[...].astype(BF16), NT_DIMS, preferred_element_type=F32)
    m = (jax.nn.gelu(act) * w_ref[...].astype(F32)).astype(BF16)
    o_ref[...] += jnp.dot(m, v_ref[...].astype(BF16), preferred_element_type=F32)


def peer_experts(n2, w_flat, u, v, *, tm=1024, te=1024):
    s, d = n2.shape
    e = u.shape[0]
    tm = min(tm, s)
    once = pl.Buffered(1)
    return pl.pallas_call(
        _peer_experts_kernel,
        grid=(s // tm, e // te),
        in_specs=[
            pl.BlockSpec((tm, d), lambda i, j: (i, 0), pipeline_mode=once),
            pl.BlockSpec((tm, te), lambda i, j: (i, j)),
            pl.BlockSpec((te, d), lambda i, j: (j, 0)),
            pl.BlockSpec((te, d), lambda i, j: (j, 0)),
        ],
        out_specs=pl.BlockSpec((tm, d), lambda i, j: (i, 0), pipeline_mode=once),
        out_shape=jax.ShapeDtypeStruct((s, d), F32),
        compiler_params=_cparams("parallel", "arbitrary"),
        name="peer_experts",
    )(n2, w_flat, u, v)


def _ple_kernel(h_ref, f_ref, p_ref, gple_ref, wg_ref, wp_ref, gfin_ref, o_ref, *, final):
    h = h_ref[...] + f_ref[...]
    n3 = _rms(h, gple_ref[...]).astype(BF16)
    gate = jax.nn.sigmoid(jnp.dot(n3, wg_ref[...].astype(BF16), preferred_element_type=F32))
    pp = jnp.dot(p_ref[...].astype(BF16), wp_ref[...].astype(BF16), preferred_element_type=F32)
    h = h + gate * pp
    if final:
        h = _rms(h, gfin_ref[...])
    o_ref[...] = h


def ple_block(h1, ffn_out, p, ple_norm, w_gate, w_proj, final_norm, *, final, tm=256):
    s, d = h1.shape
    pd = p.shape[1]
    tm = min(tm, s)
    full = lambda i: (0, 0)
    return pl.pallas_call(
        functools.partial(_ple_kernel, final=final),
        grid=(s // tm,),
        in_specs=[
            pl.BlockSpec((tm, d), lambda i: (i, 0)),
            pl.BlockSpec((tm, d), lambda i: (i, 0)),
            pl.BlockSpec((tm, pd), lambda i: (i, 0)),
            pl.BlockSpec((1, d), full),
            pl.BlockSpec((d, d), full, pipeline_mode=pl.Buffered(1)),
            pl.BlockSpec((pd, d), full, pipeline_mode=pl.Buffered(1)),
            pl.BlockSpec((1, d), full),
        ],
        out_specs=pl.BlockSpec((tm, d), lambda i: (i, 0)),
        out_shape=jax.ShapeDtypeStruct((s, d), F32),
        compiler_params=_cparams("parallel"),
        name="ple_block",
    )(h1, ffn_out, p, ple_norm.reshape(1, d), w_gate, w_proj, final_norm.reshape(1, d))


def _rot_cols(w):
    half = QK_ROPE_DIM // 2
    return jnp.concatenate([-w[..., half:], w[..., :half]], axis=-1)


def _layer(h, p, cos128, sin128, prm, *, final, final_norm):
    s, d = h.shape
    (attn_norm, w_in, b_gate, q_norm, w_uq, kv_norm, w_ukv, w_attn_o, conv_w, conv_b, w_rg_a, b_rg_a,
     w_rg_x, b_rg_x, lru_lambda, w_rnn_o, w_out, ffn_norm, w_peer_q, peer_subkeys, peer_u, peer_v,
     ple_norm, w_ple_gate, w_ple_proj) = prm

    o_q, o_kv, o_kr, o_x, o_y, o_g = 0, Q_LORA_RANK, Q_LORA_RANK + KV_LORA_RANK, \
        Q_LORA_RANK + KV_LORA_RANK + QK_ROPE_DIM, Q_LORA_RANK + KV_LORA_RANK + QK_ROPE_DIM + d, \
        Q_LORA_RANK + KV_LORA_RANK + QK_ROPE_DIM + 2 * d
    w_in_t = w_in.T
    w_kr = w_in_t[o_kr:o_x]
    w_kr_rot = _rot_cols(w_kr.T).T
    w_lat = jnp.concatenate([w_in_t[o_q:o_kr], w_kr, w_kr, w_kr_rot, w_kr_rot], axis=0).astype(BF16)
    uq = w_uq.reshape(Q_LORA_RANK, N_HEADS, QK_HEAD_DIM)
    uq_pe = uq[:, :, QK_NOPE_DIM:]
    w_q = jnp.concatenate([uq[:, :, :QK_NOPE_DIM].reshape(Q_LORA_RANK, -1),
                           uq_pe.reshape(Q_LORA_RANK, -1),
                           _rot_cols(uq_pe).reshape(Q_LORA_RANK, -1)], axis=1).astype(BF16)
    n_lat = Q_LORA_RANK + KV_LORA_RANK + 2 * LANES
    n_qn = N_HEADS * QK_NOPE_DIM
    n_qp = N_HEADS * QK_ROPE_DIM
    scale = QK_HEAD_DIM ** -0.5 * math.log2(math.e)

    n1 = rmsnorm_bf16(h, attn_norm)

    def lat_body(acc, extra, outs):
        outs[0][...] = acc[:, :Q_LORA_RANK]
        outs[1][...] = acc[:, Q_LORA_RANK:Q_LORA_RANK + KV_LORA_RANK]
        outs[2][...] = acc[:, Q_LORA_RANK + KV_LORA_RANK:]

    tm_lat = 512
    c_q, c_kv, kr4 = _mm_call(
        "in_proj_latents", n1, w_lat, tm=tm_lat, tn=n_lat, body=lat_body, w_rows_are_outputs=True,
        out_shapes=[jax.ShapeDtypeStruct((s, Q_LORA_RANK), F32), jax.ShapeDtypeStruct((s, KV_LORA_RANK), F32),
                    jax.ShapeDtypeStruct((s, 2 * LANES), F32)],
        out_specs=[pl.BlockSpec((min(tm_lat, s), Q_LORA_RANK), lambda j, i: (i, 0)),
                   pl.BlockSpec((min(tm_lat, s), KV_LORA_RANK), lambda j, i: (i, 0)),
                   pl.BlockSpec((min(tm_lat, s), 2 * LANES), lambda j, i: (i, 0))])

    tm, tn = min(1024, s), 1024

    def store_body(acc, extra, outs):
        outs[0][...] = acc.astype(outs[0].dtype)

    x_r = _mm_call("in_proj_x", n1, w_in_t, tm=tm, tn=tn, body=store_body, w_rows_are_outputs=True,
                   w_row_range=(o_x, o_y),
                   out_shapes=jax.ShapeDtypeStruct((s, d), F32), out_specs=_tile_spec(tm, tn))

    def gelu_body(acc, extra, outs):
        outs[0][...] = jax.nn.gelu(acc).astype(outs[0].dtype)

    gy = _mm_call("in_proj_y", n1, w_in_t, tm=tm, tn=tn, body=gelu_body, w_rows_are_outputs=True,
                  w_row_range=(o_y, o_g),
                  out_shapes=jax.ShapeDtypeStruct((s, d), BF16), out_specs=_tile_spec(tm, tn))

    def gate_body(acc, extra, outs):
        outs[0][...] = jax.nn.sigmoid(acc + extra[0][...]).astype(outs[0].dtype)

    gates = _mm_call("in_proj_gates", n1, w_in_t, tm=tm, tn=tn, body=gate_body, w_rows_are_outputs=True,
                     w_row_range=(o_g, o_g + 2 * d),
                     extras=(b_gate.reshape(1, 2 * d),), extra_specs=(_col_spec(tn),),
                     out_shapes=jax.ShapeDtypeStruct((s, 2 * d), BF16), out_specs=_tile_spec(tm, tn))

    def norm_prologue(a, extra):
        return _rms(a, extra[0][...]).astype(BF16)

    def q_body(acc, extra, outs):
        outs[0][...] = (acc[:, :n_qn] * scale).astype(BF16)
        cos_t = jnp.tile(extra[1][...], (1, n_qp // LANES))
        sin_t = jnp.tile(extra[2][...], (1, n_qp // LANES))
        pe = acc[:, n_qn:n_qn + n_qp] * cos_t + acc[:, n_qn + n_qp:] * sin_t
        outs[1][...] = (pe * scale).astype(BF16)

    tm_q = min(512, s)
    row128 = pl.BlockSpec((tm_q, LANES), lambda j, i: (i, 0))
    qn, qpe = _mm_call(
        "q_proj", c_q, w_q, tm=tm_q, tn=n_qn + 2 * n_qp, prologue=norm_prologue, body=q_body,
        extras=(q_norm.reshape(1, Q_LORA_RANK), cos128, sin128),
        extra_specs=(pl.BlockSpec((1, Q_LORA_RANK), lambda j, i: (0, 0)), row128, row128),
        out_shapes=[jax.ShapeDtypeStruct((s, n_qn), BF16), jax.ShapeDtypeStruct((s, n_qp), BF16)],
        out_specs=[pl.BlockSpec((tm_q, n_qn), lambda j, i: (i, 0)), pl.BlockSpec((tm_q, n_qp), lambda j, i: (i, 0))])

    def kv_body(acc, extra, outs):
        outs[0][...] = acc.astype(BF16)
        kr = extra[1][...]
        kpe2 = kr[:, :LANES] * extra[2][...] + kr[:, LANES:] * extra[3][...]
        lane = lax.broadcasted_iota(jnp.int32, kpe2.shape, 1)
        outs[1][...] = jnp.concatenate([jnp.where(lane < QK_ROPE_DIM, kpe2, 0.0),
                                        jnp.where(lane >= QK_ROPE_DIM, kpe2, 0.0)], axis=1).astype(BF16)

    n_kv = N_HEADS * (QK_NOPE_DIM + V_HEAD_DIM)
    kv, kpe = _mm_call(
        "kv_proj", c_kv, w_ukv, tm=tm_q, tn=n_kv, prologue=norm_prologue, body=kv_body,
        extras=(kv_norm.reshape(1, KV_LORA_RANK), kr4, cos128, sin128),
        extra_specs=(pl.BlockSpec((1, KV_LORA_RANK), lambda j, i: (0, 0)),
                     pl.BlockSpec((tm_q, 2 * LANES), lambda j, i: (i, 0)), row128, row128),
        out_shapes=[jax.ShapeDtypeStruct((s, n_kv), BF16), jax.ShapeDtypeStruct((s, 2 * LANES), BF16)],
        out_specs=[pl.BlockSpec((tm_q, n_kv), lambda j, i: (i, 0)), pl.BlockSpec((tm_q, 2 * LANES), lambda j, i: (i, 0))])

    o_attn = mla_attention(qn, qpe, kv, kpe)
    y_rnn_in = rglru(x_r, gy, conv_w, conv_b, w_rg_a, b_rg_a, w_rg_x, b_rg_x, lru_lambda)

    def merge_kernel(o_ref, y_ref, wo_ref, wr_ref, ga_ref, gr_ref, m_ref):
        ya = jnp.dot(o_ref[...], wo_ref[...], preferred_element_type=F32)
        yr = jnp.dot(y_ref[...], wr_ref[...], preferred_element_type=F32)
        m_ref[...] = (ga_ref[...].astype(F32) * ya + gr_ref[...].astype(F32) * yr).astype(m_ref.dtype)

    a_spec = pl.BlockSpec((tm, d), lambda j, i: (i, 0))
    w_spec = pl.BlockSpec((d, tn), lambda j, i: (0, j))
    merged = pl.pallas_call(
        merge_kernel,
        grid=(d // tn, s // tm),
        in_specs=[a_spec, a_spec, w_spec, w_spec, _tile_spec(tm, tn), _tile_spec(tm, tn, d // tn)],
        out_specs=_tile_spec(tm, tn),
        out_shape=jax.ShapeDtypeStruct((s, d), BF16),
        compiler_params=_cparams("parallel", "parallel"),
        name="mixer_merge",
    )(o_attn, y_rnn_in, w_attn_o.astype(BF16), w_rnn_o.astype(BF16), gates, gates)

    def out_body(acc, extra, outs):
        h1 = extra[0][...] + acc
        outs[0][...] = h1
        outs[1][...] = _rms(h1, extra[1][...]).astype(BF16)

    tm_o = min(512, s)
    rowd = pl.BlockSpec((tm_o, d), lambda j, i: (i, 0))
    h1, n2 = _mm_call(
        "out_proj", merged, w_out, tm=tm_o, tn=d, body=out_body,
        extras=(h, ffn_norm.reshape(1, d)), extra_specs=(rowd, pl.BlockSpec((1, d), lambda j, i: (0, 0))),
        out_shapes=[jax.ShapeDtypeStruct((s, d), F32), jax.ShapeDtypeStruct((s, d), BF16)],
        out_specs=[rowd, rowd])

    n_hc = 2 * PEER_HEADS

    def pq_body(acc, extra, outs):
        for c in range(n_hc):
            outs[0][c] = acc[:, c * PEER_HALF:(c + 1) * PEER_HALF]

    pq = _mm_call("peer_query", n2, w_peer_q, tm=tm_o, tn=n_hc * PEER_HALF, body=pq_body,
                  out_shapes=jax.ShapeDtypeStruct((n_hc, s, PEER_HALF), F32),
                  out_specs=pl.BlockSpec((n_hc, tm_o, PEER_HALF), lambda j, i: (0, i, 0)))
    w_tab = peer_route_table(pq, peer_subkeys.reshape(n_hc, N_KEYS, PEER_HALF).astype(BF16))
    ffn_out = peer_experts(n2, w_tab, peer_u, peer_v)

    return ple_block(h1, ffn_out, p, ple_norm, w_ple_gate, w_ple_proj,
                     final_norm, final=final)


def kernel(x, p, attn_norm, w_in, b_gate, q_norm, w_uq, kv_norm, w_ukv, w_attn_o, conv_w, conv_b, w_rg_a, b_rg_a,
           w_rg_x, b_rg_x, lru_lambda, w_rnn_o, w_out, ffn_norm, w_peer_q, peer_subkeys, peer_u, peer_v, ple_norm,
           w_ple_gate, w_ple_proj, final_norm):
    bsz, s, d = x.shape
    assert bsz == 1, "one sequence per call"
    depth = w_in.shape[0]
    layer_params = (attn_norm, w_in, b_gate, q_norm, w_uq, kv_norm, w_ukv, w_attn_o, conv_w, conv_b, w_rg_a,
                    b_rg_a, w_rg_x, b_rg_x, lru_lambda, w_rnn_o, w_out, ffn_norm, w_peer_q, peer_subkeys, peer_u,
                    peer_v, ple_norm, w_ple_gate, w_ple_proj)

    half = QK_ROPE_DIM // 2
    inv_freq = 1.0 / (ROPE_THETA ** (jnp.arange(half, dtype=F32) / half))
    ang = jnp.arange(s, dtype=F32)[:, None] * inv_freq[None, :]
    cos128 = jnp.tile(jnp.cos(ang), (1, LANES // half))
    sin128 = jnp.tile(jnp.sin(ang), (1, LANES // half))

    h = x.reshape(s, d)
    for l in range(depth):
        h = _layer(h, p[l, 0], cos128, sin128, tuple(w[l] for w in layer_params),
                   final=(l == depth - 1), final_norm=final_norm)
    return h.reshape(bsz, s, d)
```

```python
import functools
import math

import jax
import jax.numpy as jnp
from jax import lax
from jax.experimental import pallas as pl
from jax.experimental.pallas import tpu as pltpu

RMS_EPS = 1e-6
N_HEADS = 16
QK_NOPE_DIM = 128
QK_ROPE_DIM = 64
V_HEAD_DIM = 128
QK_HEAD_DIM = QK_NOPE_DIM + QK_ROPE_DIM
Q_LORA_RANK = 768
KV_LORA_RANK = 512
ROPE_THETA = 10000.0
RNN_BLOCKS = 16
RNN_BLOCK_DIM = 128
CONV_WIDTH = 4
LRU_C = 8.0
PEER_HEADS = 8
N_KEYS = 128
PEER_HALF = 128
PEER_TOPK = 16

LANES = 128
SUBLANES = 8
VMEM_LIMIT_BYTES = 56 * 1024 * 1024

NORM_ROWS = 512
PROJ_TILE = (1024, 1024)
LATENT_ROWS = 512
ATTN_BLOCK = 512
RGLRU_TIME_BLOCK = 512
ROUTE_TOKENS = 256
EXPERT_TILE = (1024, 1024)
PLE_ROWS = 256

F32 = jnp.float32
BF16 = jnp.bfloat16
NT_DIMS = (((1,), (1,)), ((), ()))


def _cparams(*sem):
    return pltpu.CompilerParams(dimension_semantics=sem, vmem_limit_bytes=VMEM_LIMIT_BYTES)


def _rms(x, g):
    ms = jnp.mean(x * x, axis=-1, keepdims=True)
    return (x * lax.rsqrt(ms + RMS_EPS)) * g


def _rmsnorm_kernel(x_ref, g_ref, o_ref):
    o_ref[...] = _rms(x_ref[...], g_ref[...]).astype(o_ref.dtype)


def rmsnorm_bf16(x, g, *, tr=NORM_ROWS):
    m, d = x.shape
    tr = min(tr, m)
    return pl.pallas_call(
        _rmsnorm_kernel,
        grid=(m // tr,),
        in_specs=[pl.BlockSpec((tr, d), lambda i: (i, 0)), pl.BlockSpec((1, d), lambda i: (0, 0))],
        out_specs=pl.BlockSpec((tr, d), lambda i: (i, 0)),
        out_shape=jax.ShapeDtypeStruct((m, d), BF16),
        compiler_params=_cparams("parallel"),
        name="rmsnorm",
    )(x, g.reshape(1, d))


def _mm_kernel(a_ref, w_ref, *refs, n_extra, prologue, body, w_rows_are_outputs):
    extra = refs[:n_extra]
    outs = refs[n_extra:]
    a = a_ref[...]
    if prologue is not None:
        a = prologue(a, extra)
    w = w_ref[...].astype(BF16)
    if w_rows_are_outputs:
        acc = lax.dot_general(a, w, NT_DIMS, preferred_element_type=F32)
    else:
        acc = jnp.dot(a, w, preferred_element_type=F32)
    body(acc, extra, outs)


def _mm_call(name, a, w, *, tm, tn, extras=(), extra_specs=(), out_shapes, out_specs, body, prologue=None,
             w_rows_are_outputs=False, w_row_range=None):
    m, k = a.shape
    tm = min(tm, m)
    if w_row_range is not None:
        assert w_rows_are_outputs
        row0, row1 = w_row_range
        n = row1 - row0
        assert row0 % SUBLANES == 0 and w.dtype == F32
        w_spec = pl.BlockSpec((pl.Element(tn), pl.Element(k)),
                              lambda j, i: (pl.multiple_of(row0 + j * tn, SUBLANES), 0))
    else:
        n = w.shape[0] if w_rows_are_outputs else w.shape[1]
        mode = dict(pipeline_mode=pl.Buffered(1)) if n == tn else {}
        w_spec = pl.BlockSpec((tn, k), lambda j, i: (j, 0), **mode) if w_rows_are_outputs else \
            pl.BlockSpec((k, tn), lambda j, i: (0, j), **mode)
    assert m % tm == 0 and n % tn == 0
    kern = functools.partial(_mm_kernel, n_extra=len(extras), prologue=prologue, body=body,
                             w_rows_are_outputs=w_rows_are_outputs)
    return pl.pallas_call(
        kern,
        grid=(n // tn, m // tm),
        in_specs=[pl.BlockSpec((tm, k), lambda j, i: (i, 0)), w_spec]
        + list(extra_specs),
        out_specs=out_specs,
        out_shape=out_shapes,
        compiler_params=_cparams("parallel", "parallel"),
        name=name,
    )(a, w, *extras)


def _tile_spec(tm, tn, col_off_blocks=0):
    return pl.BlockSpec((tm, tn), lambda j, i: (i, j + col_off_blocks))


def _col_spec(tn):
    return pl.BlockSpec((1, tn), lambda j, i: (0, j))


ATTN_ROW_CHUNK = 32


def _attn_kernel(qn_ref, qpe_ref, kv_ref, kpe_ref, o_ref, s0_sc, s1_sc, p0_sc, p1_sc, a0_sc, a1_sc,
                 m_sc, l_sc, acc_sc, *, tq):
    i = pl.program_id(1)
    rc = ATTN_ROW_CHUNK
    qpe = qpe_ref[...]
    qs = [jnp.concatenate([qn_ref[:, hh * LANES:(hh + 1) * LANES], qpe], axis=1) for hh in range(2)]
    m_sc[...] = jnp.full(m_sc.shape, -jnp.inf, F32)
    l_sc[...] = jnp.zeros(l_sc.shape, F32)
    acc_sc[...] = jnp.zeros(acc_sc.shape, F32)
    chunks = [slice(c * rc, (c + 1) * rc) for c in range(tq // rc)]
    even = (s0_sc, p0_sc, a0_sc)
    odd = (s1_sc, p1_sc, a1_sc)

    def scores(kb, bufs):
        s_sc = bufs[0]
        start = pl.multiple_of(kb * tq, tq)
        for hh in range(2):
            k = jnp.concatenate(
                [kv_ref[pl.ds(start, tq), hh * 256:hh * 256 + LANES],
                 kpe_ref[pl.ds(start, tq), hh * LANES:(hh + 1) * LANES]], axis=1)
            s_sc[hh] = lax.dot_general(qs[hh], k, NT_DIMS, preferred_element_type=F32)

    def accumulate(kb, bufs, diagonal=False):
        s_sc, p_sc, a_sc = bufs
        start = pl.multiple_of(kb * tq, tq)

        def visible(rows):
            if not diagonal:
                return s_sc[hh, rows, :]
            width = min(tq, -(-rows.stop // LANES) * LANES)
            row = lax.broadcasted_iota(jnp.int32, (rc, width), 0) + rows.start
            col = lax.broadcasted_iota(jnp.int32, (rc, width), 1)
            return jnp.where(col <= row, s_sc[hh, rows, 0:width], -jnp.inf)

        for hh in range(2):
            for rows in chunks:
                m_old = m_sc[hh, rows, :]
                m_new = jnp.maximum(m_old, jnp.max(visible(rows), axis=-1, keepdims=True))
                a_sc[hh, rows, :] = jnp.exp2(m_old - m_new)
                m_sc[hh, rows, :] = m_new
            for rows in chunks:
                s = visible(rows)
                width = s.shape[1]
                m_new = jnp.concatenate([m_sc[hh, rows, :]] * (width // LANES), axis=1)
                p = jnp.exp2(s - m_new)
                l_sc[hh, rows, :] = a_sc[hh, rows, :] * l_sc[hh, rows, :] + jnp.sum(p, axis=-1, keepdims=True)
                p_sc[hh, rows, 0:width] = p.astype(BF16)
                if width < tq:
                    p_sc[hh, rows, width:] = jnp.zeros((rc, tq - width), BF16)
        for hh in range(2):
            v = kv_ref[pl.ds(start, tq), hh * 256 + LANES:(hh + 1) * 256]
            acc_sc[hh] = a_sc[hh] * acc_sc[hh] + jnp.dot(p_sc[hh], v, preferred_element_type=F32)

    scores(0, even)

    def two_blocks(kk, carry):
        kb = 2 * kk + 1
        scores(kb, odd)
        accumulate(kb - 1, even)
        scores(kb + 1, even)
        accumulate(kb, odd)
        return carry

    lax.fori_loop(0, i // 2, two_blocks, 0)

    @pl.when(i % 2 == 1)
    def _():
        scores(i, odd)
        accumulate(i - 1, even)
        accumulate(i, odd, diagonal=True)

    @pl.when(i % 2 == 0)
    def _():
        accumulate(i, even, diagonal=True)

    for hh in range(2):
        o_ref[:, hh * LANES:(hh + 1) * LANES] = (acc_sc[hh] / l_sc[hh]).astype(o_ref.dtype)


def mla_attention(qn, qpe, kv, kpe, *, tq=ATTN_BLOCK):
    s = qn.shape[0]
    tq = min(tq, s)
    n_pairs = N_HEADS // 2
    return pl.pallas_call(
        functools.partial(_attn_kernel, tq=tq),
        grid=(n_pairs, s // tq),
        in_specs=[
            pl.BlockSpec((tq, 2 * LANES), lambda j, i: (i, j)),
            pl.BlockSpec((tq, LANES), lambda j, i: (i, j)),
            pl.BlockSpec((s, 4 * LANES), lambda j, i: (0, j)),
            pl.BlockSpec((s, 2 * LANES), lambda j, i: (0, 0)),
        ],
        out_specs=pl.BlockSpec((tq, 2 * LANES), lambda j, i: (i, j)),
        out_shape=jax.ShapeDtypeStruct((s, N_HEADS * V_HEAD_DIM), BF16),
        scratch_shapes=[pltpu.VMEM((2, tq, tq), F32), pltpu.VMEM((2, tq, tq), F32),
                        pltpu.VMEM((2, tq, tq), BF16), pltpu.VMEM((2, tq, tq), BF16),
                        pltpu.VMEM((2, tq, LANES), F32), pltpu.VMEM((2, tq, LANES), F32),
                        pltpu.VMEM((2, tq, LANES), F32), pltpu.VMEM((2, tq, LANES), F32),
                        pltpu.VMEM((2, tq, V_HEAD_DIM), F32)],
        compiler_params=_cparams("parallel", "parallel"),
        name="mla_attention",
    )(qn, qpe, kv, kpe)


def _softplus(z):
    return jnp.maximum(z, 0.0) + jnp.log1p(jnp.exp(-jnp.abs(z)))


RGLRU_BLOCKS_PER_STEP = 4


def _rglru_kernel(x_ref, gy_ref, cw_ref, cb_ref, wa_ref, ba_ref, wx_ref, bx_ref, lam_ref, y_ref,
                  h_sc, edge_sc, a_sc, b_sc, *, ts):
    tb = pl.program_id(1)

    @pl.when(tb == 0)
    def _():
        h_sc[...] = jnp.zeros_like(h_sc)
        edge_sc[0:SUBLANES, :] = jnp.zeros((SUBLANES, edge_sc.shape[1]), F32)
        a_sc[0:SUBLANES, :] = jnp.ones((SUBLANES, a_sc.shape[1]), F32)
        b_sc[0:SUBLANES, :] = jnp.zeros((SUBLANES, b_sc.shape[1]), F32)

    row8 = lax.broadcasted_iota(jnp.int32, (SUBLANES, LANES), 0)
    for c in range(RGLRU_BLOCKS_PER_STEP):
        ln = slice(c * LANES, (c + 1) * LANES)
        x = x_ref[:, ln]
        cw = cw_ref[:, ln]
        edge_sc[SUBLANES:2 * SUBLANES, ln] = x[0:SUBLANES, :]
        xc = x * cw[CONV_WIDTH - 1:CONV_WIDTH, :] + cb_ref[:, ln]
        for d in range(1, CONV_WIDTH):
            xs = jnp.concatenate([edge_sc[SUBLANES - d:2 * SUBLANES - d, ln], x_ref[SUBLANES - d:ts - d, ln]], axis=0)
            xc = xc + xs * cw[CONV_WIDTH - 1 - d:CONV_WIDTH - d, :]
        edge_sc[0:SUBLANES, ln] = x[ts - SUBLANES:, :]

        xcb = xc.astype(BF16)
        r = jax.nn.sigmoid(jnp.dot(xcb, wa_ref[c], preferred_element_type=F32) + ba_ref[c])
        ig = jax.nn.sigmoid(jnp.dot(xcb, wx_ref[c], preferred_element_type=F32) + bx_ref[c])
        log_a = (-LRU_C) * r * _softplus(-lam_ref[:, ln])
        a = jnp.exp(log_a)
        th = jnp.tanh(log_a)
        m2 = -2.0 * th / (1.0 - th)
        mult = jnp.where(m2 > 0.0, m2 * lax.rsqrt(m2), 0.0)
        gated = ig * xc
        b = mult * gated
        first = jnp.where((row8 == 0) & (tb == 0), gated[0:SUBLANES, :], b[0:SUBLANES, :])
        b = jnp.concatenate([first, b[SUBLANES:, :]], axis=0)

        d = 1
        while d < ts:
            if d < SUBLANES:
                a_sc[SUBLANES:, ln] = a
                b_sc[SUBLANES:, ln] = b
                a_sh = a_sc[SUBLANES - d:SUBLANES - d + ts, ln]
                b_sh = b_sc[SUBLANES - d:SUBLANES - d + ts, ln]
            else:
                a_sh = jnp.concatenate([jnp.ones((d, LANES), F32), a[:ts - d]], axis=0)
                b_sh = jnp.concatenate([jnp.zeros((d, LANES), F32), b[:ts - d]], axis=0)
            b = a * b_sh + b
            a = a * a_sh
            d *= 2
        h = a * h_sc[0:1, ln] + b
        h_sc[:, ln] = jnp.broadcast_to(h[ts - 1:ts, :], (SUBLANES, LANES))
        y_ref[:, ln] = (h * gy_ref[:, ln].astype(F32)).astype(y_ref.dtype)


def rglru(x_r, gy, conv_w, conv_b, w_rg_a, b_rg_a, w_rg_x, b_rg_x, lru_lambda, *, ts=RGLRU_TIME_BLOCK):
    s, c = x_r.shape
    ts = min(ts, s)
    nblk = RGLRU_BLOCKS_PER_STEP
    wide = nblk * LANES
    n_gate_blocks = c // RNN_BLOCK_DIM
    col = lambda cb, tb: (0, cb)
    blk = lambda cb, tb: (cb, 0, 0)
    return pl.pallas_call(
        functools.partial(_rglru_kernel, ts=ts),
        grid=(c // wide, s // ts),
        in_specs=[
            pl.BlockSpec((ts, wide), lambda cb, tb: (tb, cb)),
            pl.BlockSpec((ts, wide), lambda cb, tb: (tb, cb)),
            pl.BlockSpec((CONV_WIDTH, wide), col),
            pl.BlockSpec((1, wide), col),
            pl.BlockSpec((nblk, RNN_BLOCK_DIM, RNN_BLOCK_DIM), blk),
            pl.BlockSpec((nblk, 1, RNN_BLOCK_DIM), blk),
            pl.BlockSpec((nblk, RNN_BLOCK_DIM, RNN_BLOCK_DIM), blk),
            pl.BlockSpec((nblk, 1, RNN_BLOCK_DIM), blk),
            pl.BlockSpec((1, wide), col),
        ],
        out_specs=pl.BlockSpec((ts, wide), lambda cb, tb: (tb, cb)),
        out_shape=jax.ShapeDtypeStruct((s, c), BF16),
        scratch_shapes=[pltpu.VMEM((SUBLANES, wide), F32), pltpu.VMEM((2 * SUBLANES, wide), F32),
                        pltpu.VMEM((SUBLANES + ts, wide), F32), pltpu.VMEM((SUBLANES + ts, wide), F32)],
        compiler_params=_cparams("parallel", "arbitrary"),
        name="rglru",
    )(x_r, gy, conv_w, conv_b.reshape(1, c), w_rg_a.astype(BF16), b_rg_a.reshape(n_gate_blocks, 1, RNN_BLOCK_DIM),
      w_rg_x.astype(BF16), b_rg_x.reshape(n_gate_blocks, 1, RNN_BLOCK_DIM), lru_lambda.reshape(1, c))


def _sorting_network(n):
    pairs = []

    def merge(lo, length, r):
        step = r * 2
        if step < length:
            merge(lo, length, step)
            merge(lo + r, length, step)
            pairs.extend((i, i + r) for i in range(lo + r, lo + length - r, step))
        else:
            pairs.append((lo, lo + r))

    def sort(lo, length):
        if length > 1:
            half = length // 2
            sort(lo, half)
            sort(lo + half, half)
            merge(lo, length, 1)

    sort(0, n)
    return pairs


def _topk_keys(s, k, vals_ref, idx_ref):
    n_slabs = s.shape[0] // SUBLANES
    assert n_slabs == k
    sub = lax.broadcasted_iota(jnp.int32, (SUBLANES, s.shape[1]), 0)
    vals = [s[v * SUBLANES:(v + 1) * SUBLANES, :] for v in range(n_slabs)]
    idxs = [sub + v * SUBLANES for v in range(n_slabs)]
    for i, j in _sorting_network(n_slabs):
        first = (vals[i] > vals[j]) | ((vals[i] == vals[j]) & (idxs[i] < idxs[j]))
        vals[i], vals[j] = jnp.where(first, vals[i], vals[j]), jnp.where(first, vals[j], vals[i])
        idxs[i], idxs[j] = jnp.where(first, idxs[i], idxs[j]), jnp.where(first, idxs[j], idxs[i])
    for r in range(k):
        m = jnp.max(vals[0], axis=0, keepdims=True)
        im = jnp.min(jnp.where(vals[0] == m, idxs[0], N_KEYS), axis=0, keepdims=True)
        vals_ref[r:r + 1, :] = m
        idx_ref[r:r + 1, :] = im
        taken = idxs[0] == im
        for v in range(k - 1 - r):
            vals[v] = jnp.where(taken, vals[v + 1], vals[v])
            idxs[v] = jnp.where(taken, idxs[v + 1], idxs[v])


def _topk_pairs(v0_sc, v1_sc, vals_ref, flat_ref):
    k = PEER_TOPK
    tt = v0_sc.shape[1]
    row8 = lax.broadcasted_iota(jnp.int32, (SUBLANES, tt), 0)
    v0_lo = v0_sc[0:SUBLANES, :]
    lists = [jnp.where(row8 < k // (b + 1), v0_lo + v1_sc[b:b + 1, :], -jnp.inf) for b in range(k)]
    single = v0_sc[SUBLANES:k, :] + v1_sc[0:1, :]
    pos = jnp.zeros((SUBLANES, tt), jnp.int32)
    flat_single = (row8 + SUBLANES) * k
    for r in range(k):
        flat_head = row8 * k + pos
        m = jnp.max(jnp.maximum(lists[0], single), axis=0, keepdims=True)
        best = jnp.min(jnp.minimum(jnp.where(lists[0] == m, flat_head, k * k),
                                   jnp.where(single == m, flat_single, k * k)), axis=0, keepdims=True)
        vals_ref[r:r + 1, :] = m
        flat_ref[r:r + 1, :] = best
        taken = flat_head == best
        single = jnp.where(flat_single == best, -jnp.inf, single)
        pos = pos + taken.astype(jnp.int32)
        for v in range(k - 1 - r):
            lists[v] = jnp.where(taken, lists[v + 1], lists[v])


def _route_head(pq_ref, sk_ref, i1_ref, i2_ref, g_ref, v0_sc, i0_sc, v1_sc, j1_sc, bs_sc, bc_sc):
    k = PEER_TOPK
    for c, (v_sc, i_sc) in enumerate(((v0_sc, i0_sc), (v1_sc, j1_sc))):
        s_t = lax.dot_general(sk_ref[c], pq_ref[c].astype(BF16), NT_DIMS, preferred_element_type=F32)
        _topk_keys(s_t, k, v_sc, i_sc)
    _topk_pairs(v0_sc, v1_sc, bs_sc, bc_sc)
    bs = bs_sc[...]
    bc = bc_sc[...]
    row = lax.broadcasted_iota(jnp.int32, bs.shape, 0)
    i0 = i0_sc[...]
    j1 = j1_sc[...]
    for r in range(k):
        ca = bc[r:r + 1, :] >> 4
        cb = bc[r:r + 1, :] & (k - 1)
        i1_ref[r:r + 1, :] = jnp.sum(jnp.where(row == ca, i0, 0), axis=0, keepdims=True)
        i2_ref[r:r + 1, :] = jnp.sum(jnp.where(row == cb, j1, 0), axis=0, keepdims=True)
    e = jnp.exp(bs - bs[0:1, :])
    g_ref[...] = e / jnp.sum(e, axis=0, keepdims=True)


TABLE_GROUP = 16
TABLE_PITCH = N_KEYS + SUBLANES
TABLE_GROUPS_PER_STEP = 8


def _peer_route_table_kernel(pq_ref, sk_ref, w_ref, r1_sc, r2_sc, rg_sc, v0_sc, i0_sc, v1_sc, j1_sc, bs_sc, bc_sc,
                             i1_sc, i2_sc, g_sc, stage_sc, *, tw):
    i = pl.program_id(0)

    @pl.when(i == 0)
    def _():
        r1_sc[...] = jnp.zeros_like(r1_sc)
        r2_sc[...] = jnp.zeros_like(r2_sc)
        rg_sc[...] = jnp.zeros_like(rg_sc)

    cur = lax.rem(i, 2)
    prev = 1 - cur
    i1_sc[...] = r1_sc[prev].T
    i2_sc[...] = r2_sc[prev].T
    g_sc[...] = rg_sc[prev].T
    sub = lax.broadcasted_iota(jnp.int32, (N_KEYS, PEER_HEADS * PEER_TOPK), 0)
    n_iter = tw // (TABLE_GROUP * TABLE_GROUPS_PER_STEP)
    heads_per_iter = PEER_HEADS // n_iter
    k = PEER_TOPK

    def body(gi, carry):
        for sg in range(TABLE_GROUPS_PER_STEP):
            t0 = pl.multiple_of((gi * TABLE_GROUPS_PER_STEP + sg) * TABLE_GROUP, TABLE_GROUP)
            base = sg * TABLE_GROUP * TABLE_PITCH
            for u in range(TABLE_GROUP):
                r1 = i1_sc[pl.ds(t0 + u, 1), :]
                r2 = i2_sc[pl.ds(t0 + u, 1), :]
                gg = g_sc[pl.ds(t0 + u, 1), :]
                a = jnp.where(sub == r1, gg, 0.0).astype(BF16)
                b = jnp.where(sub == r2, 1.0, 0.0).astype(BF16)
                row0 = base + u * TABLE_PITCH
                stage_sc[row0:row0 + N_KEYS, :] = lax.dot_general(
                    a, b, NT_DIMS, preferred_element_type=F32)
            for k1 in range(N_KEYS):
                blk = stage_sc[pl.ds(base + k1, TABLE_GROUP, stride=TABLE_PITCH), :]
                w_ref[pl.ds(t0, TABLE_GROUP), k1 * N_KEYS:(k1 + 1) * N_KEYS] = blk.astype(w_ref.dtype)
        for hh in range(heads_per_iter):
            h = gi * heads_per_iter + hh
            rows = pl.ds(pl.multiple_of(h * k, k), k)
            _route_head(pq_ref.at[pl.ds(2 * h, 2)], sk_ref.at[pl.ds(2 * h, 2)],
                               r1_sc.at[cur, rows], r2_sc.at[cur, rows], rg_sc.at[cur, rows],
                               v0_sc, i0_sc, v1_sc, j1_sc, bs_sc, bc_sc)
        return carry

    lax.fori_loop(0, n_iter, body, 0)


def peer_route_table(pq, subkeys, *, tw=ROUTE_TOKENS):
    s = pq.shape[1]
    tw = min(tw, s)
    n_tiles = s // tw
    k = PEER_TOPK
    hk = PEER_HEADS * k
    assert tw % (TABLE_GROUP * TABLE_GROUPS_PER_STEP) == 0
    assert PEER_HEADS % (tw // (TABLE_GROUP * TABLE_GROUPS_PER_STEP)) == 0
    return pl.pallas_call(
        functools.partial(_peer_route_table_kernel, tw=tw),
        grid=(n_tiles + 1,),
        in_specs=[pl.BlockSpec((2 * PEER_HEADS, tw, PEER_HALF), lambda i: (0, jnp.minimum(i, n_tiles - 1), 0)),
                  pl.BlockSpec((2 * PEER_HEADS, N_KEYS, PEER_HALF), lambda i: (0, 0, 0))],
        out_specs=pl.BlockSpec((tw, N_KEYS * N_KEYS), lambda i: (jnp.maximum(i - 1, 0), 0)),
        out_shape=jax.ShapeDtypeStruct((s, N_KEYS * N_KEYS), BF16),
        scratch_shapes=[pltpu.VMEM((2, hk, tw), jnp.int32), pltpu.VMEM((2, hk, tw), jnp.int32),
                        pltpu.VMEM((2, hk, tw), F32),
                        pltpu.VMEM((k, tw), F32), pltpu.VMEM((k, tw), jnp.int32),
                        pltpu.VMEM((k, tw), F32), pltpu.VMEM((k, tw), jnp.int32),
                        pltpu.VMEM((k, tw), F32), pltpu.VMEM((k, tw), jnp.int32),
                        pltpu.VMEM((tw, hk), jnp.int32), pltpu.VMEM((tw, hk), jnp.int32), pltpu.VMEM((tw, hk), F32),
                        pltpu.VMEM((TABLE_GROUPS_PER_STEP * TABLE_GROUP * TABLE_PITCH, N_KEYS), F32)],
        compiler_params=_cparams("arbitrary"),
        name="peer_route_table",
    )(pq, subkeys)


def _peer_experts_kernel(n_ref, w_ref, u_ref, v_ref, o_ref):
    j = pl.program_id(1)

    @pl.when(j == 0)
    def _():
        o_ref[...] = jnp.zeros_like(o_ref)

    act = lax.dot_general(n_ref[...], u_ref[...].astype(BF16), NT_DIMS, preferred_element_type=F32)
    m = (jax.nn.gelu(act) * w_ref[...].astype(F32)).astype(BF16)
    o_ref[...] += jnp.dot(m, v_ref[...].astype(BF16), preferred_element_type=F32)


def peer_experts(n2, w_flat, u, v, *, tm=EXPERT_TILE[0], te=EXPERT_TILE[1]):
    s, d = n2.shape
    e = u.shape[0]
    tm = min(tm, s)
    once = pl.Buffered(1)
    return pl.pallas_call(
        _peer_experts_kernel,
        grid=(s // tm, e // te),
        in_specs=[
            pl.BlockSpec((tm, d), lambda i, j: (i, 0), pipeline_mode=once),
            pl.BlockSpec((tm, te), lambda i, j: (i, j)),
            pl.BlockSpec((te, d), lambda i, j: (j, 0)),
            pl.BlockSpec((te, d), lambda i, j: (j, 0)),
        ],
        out_specs=pl.BlockSpec((tm, d), lambda i, j: (i, 0), pipeline_mode=once),
        out_shape=jax.ShapeDtypeStruct((s, d), F32),
        compiler_params=_cparams("parallel", "arbitrary"),
        name="peer_experts",
    )(n2, w_flat, u, v)


def _ple_kernel(h_ref, f_ref, p_ref, gple_ref, wg_ref, wp_ref, gfin_ref, o_ref, *, final):
    h = h_ref[...] + f_ref[...]
    n3 = _rms(h, gple_ref[...]).astype(BF16)
    gate = jax.nn.sigmoid(jnp.dot(n3, wg_ref[...].astype(BF16), preferred_element_type=F32))
    pp = jnp.dot(p_ref[...].astype(BF16), wp_ref[...].astype(BF16), preferred_element_type=F32)
    h = h + gate * pp
    if final:
        h = _rms(h, gfin_ref[...])
    o_ref[...] = h


def ple_block(h1, ffn_out, p, ple_norm, w_gate, w_proj, final_norm, *, final, tm=PLE_ROWS):
    s, d = h1.shape
    pd = p.shape[1]
    tm = min(tm, s)
    full = lambda i: (0, 0)
    return pl.pallas_call(
        functools.partial(_ple_kernel, final=final),
        grid=(s // tm,),
        in_specs=[
            pl.BlockSpec((tm, d), lambda i: (i, 0)),
            pl.BlockSpec((tm, d), lambda i: (i, 0)),
            pl.BlockSpec((tm, pd), lambda i: (i, 0)),
            pl.BlockSpec((1, d), full),
            pl.BlockSpec((d, d), full, pipeline_mode=pl.Buffered(1)),
            pl.BlockSpec((pd, d), full, pipeline_mode=pl.Buffered(1)),
            pl.BlockSpec((1, d), full),
        ],
        out_specs=pl.BlockSpec((tm, d), lambda i: (i, 0)),
        out_shape=jax.ShapeDtypeStruct((s, d), F32),
        compiler_params=_cparams("parallel"),
        name="ple_block",
    )(h1, ffn_out, p, ple_norm.reshape(1, d), w_gate, w_proj, final_norm.reshape(1, d))


def _rot_cols(w):
    half = QK_ROPE_DIM // 2
    return jnp.concatenate([-w[..., half:], w[..., :half]], axis=-1)


def _layer(h, p, cos128, sin128, prm, *, final, final_norm):
    s, d = h.shape
    (attn_norm, w_in, b_gate, q_norm, w_uq, kv_norm, w_ukv, w_attn_o, conv_w, conv_b, w_rg_a, b_rg_a,
     w_rg_x, b_rg_x, lru_lambda, w_rnn_o, w_out, ffn_norm, w_peer_q, peer_subkeys, peer_u, peer_v,
     ple_norm, w_ple_gate, w_ple_proj) = prm

    o_q, o_kv, o_kr, o_x, o_y, o_g = 0, Q_LORA_RANK, Q_LORA_RANK + KV_LORA_RANK, \
        Q_LORA_RANK + KV_LORA_RANK + QK_ROPE_DIM, Q_LORA_RANK + KV_LORA_RANK + QK_ROPE_DIM + d, \
        Q_LORA_RANK + KV_LORA_RANK + QK_ROPE_DIM + 2 * d
    w_in_t = w_in.T
    w_kr = w_in_t[o_kr:o_x]
    w_kr_rot = _rot_cols(w_kr.T).T
    w_lat = jnp.concatenate([w_in_t[o_q:o_kr], w_kr, w_kr, w_kr_rot, w_kr_rot], axis=0).astype(BF16)
    uq = w_uq.reshape(Q_LORA_RANK, N_HEADS, QK_HEAD_DIM)
    uq_pe = uq[:, :, QK_NOPE_DIM:]
    w_q = jnp.concatenate([uq[:, :, :QK_NOPE_DIM].reshape(Q_LORA_RANK, -1),
                           uq_pe.reshape(Q_LORA_RANK, -1),
                           _rot_cols(uq_pe).reshape(Q_LORA_RANK, -1)], axis=1).astype(BF16)
    n_lat = Q_LORA_RANK + KV_LORA_RANK + 2 * LANES
    n_qn = N_HEADS * QK_NOPE_DIM
    n_qp = N_HEADS * QK_ROPE_DIM
    scale = QK_HEAD_DIM ** -0.5 * math.log2(math.e)

    n1 = rmsnorm_bf16(h, attn_norm)

    def lat_body(acc, extra, outs):
        outs[0][...] = acc[:, :Q_LORA_RANK]
        outs[1][...] = acc[:, Q_LORA_RANK:Q_LORA_RANK + KV_LORA_RANK]
        outs[2][...] = acc[:, Q_LORA_RANK + KV_LORA_RANK:]

    tm_lat = LATENT_ROWS
    c_q, c_kv, kr4 = _mm_call(
        "in_proj_latents", n1, w_lat, tm=tm_lat, tn=n_lat, body=lat_body, w_rows_are_outputs=True,
        out_shapes=[jax.ShapeDtypeStruct((s, Q_LORA_RANK), F32), jax.ShapeDtypeStruct((s, KV_LORA_RANK), F32),
                    jax.ShapeDtypeStruct((s, 2 * LANES), F32)],
        out_specs=[pl.BlockSpec((min(tm_lat, s), Q_LORA_RANK), lambda j, i: (i, 0)),
                   pl.BlockSpec((min(tm_lat, s), KV_LORA_RANK), lambda j, i: (i, 0)),
                   pl.BlockSpec((min(tm_lat, s), 2 * LANES), lambda j, i: (i, 0))])

    tm, tn = min(PROJ_TILE[0], s), PROJ_TILE[1]

    def store_body(acc, extra, outs):
        outs[0][...] = acc.astype(outs[0].dtype)

    x_r = _mm_call("in_proj_x", n1, w_in_t, tm=tm, tn=tn, body=store_body, w_rows_are_outputs=True,
                   w_row_range=(o_x, o_y),
                   out_shapes=jax.ShapeDtypeStruct((s, d), F32), out_specs=_tile_spec(tm, tn))

    def gelu_body(acc, extra, outs):
        outs[0][...] = jax.nn.gelu(acc).astype(outs[0].dtype)

    gy = _mm_call("in_proj_y", n1, w_in_t, tm=tm, tn=tn, body=gelu_body, w_rows_are_outputs=True,
                  w_row_range=(o_y, o_g),
                  out_shapes=jax.ShapeDtypeStruct((s, d), BF16), out_specs=_tile_spec(tm, tn))

    def gate_body(acc, extra, outs):
        outs[0][...] = jax.nn.sigmoid(acc + extra[0][...]).astype(outs[0].dtype)

    gates = _mm_call("in_proj_gates", n1, w_in_t, tm=tm, tn=tn, body=gate_body, w_rows_are_outputs=True,
                     w_row_range=(o_g, o_g + 2 * d),
                     extras=(b_gate.reshape(1, 2 * d),), extra_specs=(_col_spec(tn),),
                     out_shapes=jax.ShapeDtypeStruct((s, 2 * d), BF16), out_specs=_tile_spec(tm, tn))

    def norm_prologue(a, extra):
        return _rms(a, extra[0][...]).astype(BF16)

    def q_body(acc, extra, outs):
        outs[0][...] = (acc[:, :n_qn] * scale).astype(BF16)
        cos_t = jnp.tile(extra[1][...], (1, n_qp // LANES))
        sin_t = jnp.tile(extra[2][...], (1, n_qp // LANES))
        pe = acc[:, n_qn:n_qn + n_qp] * cos_t + acc[:, n_qn + n_qp:] * sin_t
        outs[1][...] = (pe * scale).astype(BF16)

    tm_q = min(LATENT_ROWS, s)
    row128 = pl.BlockSpec((tm_q, LANES), lambda j, i: (i, 0))
    qn, qpe = _mm_call(
        "q_proj", c_q, w_q, tm=tm_q, tn=n_qn + 2 * n_qp, prologue=norm_prologue, body=q_body,
        extras=(q_norm.reshape(1, Q_LORA_RANK), cos128, sin128),
        extra_specs=(pl.BlockSpec((1, Q_LORA_RANK), lambda j, i: (0, 0)), row128, row128),
        out_shapes=[jax.ShapeDtypeStruct((s, n_qn), BF16), jax.ShapeDtypeStruct((s, n_qp), BF16)],
        out_specs=[pl.BlockSpec((tm_q, n_qn), lambda j, i: (i, 0)), pl.BlockSpec((tm_q, n_qp), lambda j, i: (i, 0))])

    def kv_body(acc, extra, outs):
        outs[0][...] = acc.astype(BF16)
        kr = extra[1][...]
        kpe2 = kr[:, :LANES] * extra[2][...] + kr[:, LANES:] * extra[3][...]
        lane = lax.broadcasted_iota(jnp.int32, kpe2.shape, 1)
        outs[1][...] = jnp.concatenate([jnp.where(lane < QK_ROPE_DIM, kpe2, 0.0),
                                        jnp.where(lane >= QK_ROPE_DIM, kpe2, 0.0)], axis=1).astype(BF16)

    n_kv = N_HEADS * (QK_NOPE_DIM + V_HEAD_DIM)
    kv, kpe = _mm_call(
        "kv_proj", c_kv, w_ukv, tm=tm_q, tn=n_kv, prologue=norm_prologue, body=kv_body,
        extras=(kv_norm.reshape(1, KV_LORA_RANK), kr4, cos128, sin128),
        extra_specs=(pl.BlockSpec((1, KV_LORA_RANK), lambda j, i: (0, 0)),
                     pl.BlockSpec((tm_q, 2 * LANES), lambda j, i: (i, 0)), row128, row128),
        out_shapes=[jax.ShapeDtypeStruct((s, n_kv), BF16), jax.ShapeDtypeStruct((s, 2 * LANES), BF16)],
        out_specs=[pl.BlockSpec((tm_q, n_kv), lambda j, i: (i, 0)), pl.BlockSpec((tm_q, 2 * LANES), lambda j, i: (i, 0))])

    o_attn = mla_attention(qn, qpe, kv, kpe)
    y_rnn_in = rglru(x_r, gy, conv_w, conv_b, w_rg_a, b_rg_a, w_rg_x, b_rg_x, lru_lambda)

    def merge_kernel(o_ref, y_ref, wo_ref, wr_ref, ga_ref, gr_ref, m_ref):
        ya = jnp.dot(o_ref[...], wo_ref[...], preferred_element_type=F32)
        yr = jnp.dot(y_ref[...], wr_ref[...], preferred_element_type=F32)
        m_ref[...] = (ga_ref[...].astype(F32) * ya + gr_ref[...].astype(F32) * yr).astype(m_ref.dtype)

    a_spec = pl.BlockSpec((tm, d), lambda j, i: (i, 0))
    w_spec = pl.BlockSpec((d, tn), lambda j, i: (0, j))
    merged = pl.pallas_call(
        merge_kernel,
        grid=(d // tn, s // tm),
        in_specs=[a_spec, a_spec, w_spec, w_spec, _tile_spec(tm, tn), _tile_spec(tm, tn, d // tn)],
        out_specs=_tile_spec(tm, tn),
        out_shape=jax.ShapeDtypeStruct((s, d), BF16),
        compiler_params=_cparams("parallel", "parallel"),
        name="mixer_merge",
    )(o_attn, y_rnn_in, w_attn_o.astype(BF16), w_rnn_o.astype(BF16), gates, gates)

    def out_body(acc, extra, outs):
        h1 = extra[0][...] + acc
        outs[0][...] = h1
        outs[1][...] = _rms(h1, extra[1][...]).astype(BF16)

    tm_o = min(LATENT_ROWS, s)
    rowd = pl.BlockSpec((tm_o, d), lambda j, i: (i, 0))
    h1, n2 = _mm_call(
        "out_proj", merged, w_out, tm=tm_o, tn=d, body=out_body,
        extras=(h, ffn_norm.reshape(1, d)), extra_specs=(rowd, pl.BlockSpec((1, d), lambda j, i: (0, 0))),
        out_shapes=[jax.ShapeDtypeStruct((s, d), F32), jax.ShapeDtypeStruct((s, d), BF16)],
        out_specs=[rowd, rowd])

    n_hc = 2 * PEER_HEADS

    def pq_body(acc, extra, outs):
        for c in range(n_hc):
            outs[0][c] = acc[:, c * PEER_HALF:(c + 1) * PEER_HALF]

    pq = _mm_call("peer_query", n2, w_peer_q, tm=tm_o, tn=n_hc * PEER_HALF, body=pq_body,
                  out_shapes=jax.ShapeDtypeStruct((n_hc, s, PEER_HALF), F32),
                  out_specs=pl.BlockSpec((n_hc, tm_o, PEER_HALF), lambda j, i: (0, i, 0)))
    w_tab = peer_route_table(pq, peer_subkeys.reshape(n_hc, N_KEYS, PEER_HALF).astype(BF16))
    ffn_out = peer_experts(n2, w_tab, peer_u, peer_v)

    return ple_block(h1, ffn_out, p, ple_norm, w_ple_gate, w_ple_proj,
                     final_norm, final=final)


def kernel(x, p, attn_norm, w_in, b_gate, q_norm, w_uq, kv_norm, w_ukv, w_attn_o, conv_w, conv_b, w_rg_a, b_rg_a,
           w_rg_x, b_rg_x, lru_lambda, w_rnn_o, w_out, ffn_norm, w_peer_q, peer_subkeys, peer_u, peer_v, ple_norm,
           w_ple_gate, w_ple_proj, final_norm):
    bsz, s, d = x.shape
    assert bsz == 1, "one sequence per call"
    depth = w_in.shape[0]
    layer_params = (attn_norm, w_in, b_gate, q_norm, w_uq, kv_norm, w_ukv, w_attn_o, conv_w, conv_b, w_rg_a,
                    b_rg_a, w_rg_x, b_rg_x, lru_lambda, w_rnn_o, w_out, ffn_norm, w_peer_q, peer_subkeys, peer_u,
                    peer_v, ple_norm, w_ple_gate, w_ple_proj)

    half = QK_ROPE_DIM // 2
    inv_freq = 1.0 / (ROPE_THETA ** (jnp.arange(half, dtype=F32) / half))
    ang = jnp.arange(s, dtype=F32)[:, None] * inv_freq[None, :]
    cos128 = jnp.tile(jnp.cos(ang), (1, LANES // half))
    sin128 = jnp.tile(jnp.sin(ang), (1, LANES // half))

    h = x.reshape(s, d)
    for l in range(depth):
        h = _layer(h, p[l, 0], cos128, sin128, tuple(w[l] for w in layer_params),
                   final=(l == depth - 1), final_norm=final_norm)
    return h.reshape(bsz, s, d)
```
